```python
import math
import jax, jax.numpy as jnp
from jax import lax
import numpy as np

D_MODEL = 4096
BATCH = 4
SEQ = 4096
DEPTH = 1

ATTN_HEADS = 8
ATTN_HEAD_DIM = 128
ATTN_WIDTH = ATTN_HEADS * 2 * ATTN_HEAD_DIM
CONV_WIDTH = D_MODEL - ATTN_WIDTH
CONV_GROUPS = 16
CONV_K = 3
IN_WIDTH = 3 * ATTN_WIDTH + 3 * CONV_WIDTH
D_FF = 11008
ROPE_THETA = 10000.0
Q_BLOCK = 128
EPS = 1e-6
N_MOD = 6

kernel_name = "hybrid_diffattn_shortconv_convffn_block"


def rmsnorm(x, g):
    xf = x.astype(jnp.float32)
    y = xf * lax.rsqrt(jnp.mean(xf * xf, axis=-1, keepdims=True) + EPS) * g.astype(jnp.float32)
    return y.astype(x.dtype)


def causal_dwconv(x, w):
    k = w.shape[0]
    s = x.shape[1]
    xp = jnp.pad(x, ((0, 0), (k - 1, 0), (0, 0)))
    y = xp[:, 0:s] * w[0]
    for j in range(1, k):
        y = y + xp[:, j:j + s] * w[j]
    return y


def rope_tables(positions, dim):
    inv_freq = ROPE_THETA ** (-jnp.arange(0, dim, 2, dtype=jnp.float32) / dim)
    ang = positions.astype(jnp.float32)[..., None] * inv_freq
    ang = jnp.concatenate([ang, ang], axis=-1)
    return jnp.cos(ang), jnp.sin(ang)


def apply_rope(t, cos, sin):
    cos = cos[:, :, None, None, :].astype(t.dtype)
    sin = sin[:, :, None, None, :].astype(t.dtype)
    half = t.shape[-1] // 2
    rot = jnp.concatenate([-t[..., half:], t[..., :half]], axis=-1)
    return t * cos + rot * sin


def diff_attention(q, k, v, lam):
    b, s, h, _, d = q.shape
    nb = s // Q_BLOCK
    scale = d ** -0.5
    qb = q.reshape(b, nb, Q_BLOCK, h, 2, d).transpose(1, 0, 2, 3, 4, 5)
    kpos = jnp.arange(s)

    def one_block(args):
        qi, i = args
        sc = jnp.einsum('bqhcd,bkhcd->bchqk', qi, k).astype(jnp.float32) * scale
        qpos = i * Q_BLOCK + jnp.arange(Q_BLOCK)
        mask = kpos[None, :] <= qpos[:, None]
        sc = jnp.where(mask, sc, jnp.finfo(jnp.float32).min)
        p = jax.nn.softmax(sc, axis=-1)
        a = p[:, 0] - lam * p[:, 1]
        return jnp.einsum('bhqk,bkhe->bqhe', a.astype(v.dtype), v)

    out = lax.map(one_block, (qb, jnp.arange(nb)))
    return out.transpose(1, 0, 2, 3, 4).reshape(b, s, h, 2 * d)


def setup_inputs(seed: int = 0) -> dict:
    key = jax.random.key(seed)
    ks = jax.random.split(key, 20)
    f32 = jnp.float32
    nrm = lambda k, shape, s: jax.random.normal(k, shape, f32) * s
    x = jax.random.normal(ks[0], (BATCH, SEQ, D_MODEL), f32)
    c = jax.random.normal(ks[1], (BATCH, D_MODEL), f32)
    offset = jax.random.randint(ks[2], (BATCH,), 0, 1024, dtype=jnp.int32)
    positions = offset[:, None] + jnp.arange(SEQ, dtype=jnp.int32)[None, :]
    return {
        "x": x,
        "c": c,
        "positions": positions,
        "w_ada": nrm(ks[3], (DEPTH, D_MODEL, N_MOD * D_MODEL), D_MODEL ** -0.5),
        "b_ada": nrm(ks[4], (DEPTH, N_MOD * D_MODEL), 0.02),
        "g_pre_mix": 1.0 + nrm(ks[5], (DEPTH, D_MODEL), 0.02),
        "g_post_mix": 1.0 + nrm(ks[6], (DEPTH, D_MODEL), 0.02),
        "w_in": nrm(ks[7], (DEPTH, D_MODEL, IN_WIDTH), D_MODEL ** -0.5),
        "lambda_q1": nrm(ks[8], (DEPTH, ATTN_HEAD_DIM), 0.1),
        "lambda_k1": nrm(ks[9], (DEPTH, ATTN_HEAD_DIM), 0.1),
        "lambda_q2": nrm(ks[10], (DEPTH, ATTN_HEAD_DIM), 0.1),
        "lambda_k2": nrm(ks[11], (DEPTH, ATTN_HEAD_DIM), 0.1),
        "g_subln": 1.0 + nrm(ks[12], (DEPTH, 2 * ATTN_HEAD_DIM), 0.02),
        "w_conv_mix": nrm(ks[13], (DEPTH, CONV_K, CONV_WIDTH), CONV_K ** -0.5),
        "w_out": nrm(ks[14], (DEPTH, D_MODEL, D_MODEL), D_MODEL ** -0.5),
        "g_pre_ffn": 1.0 + nrm(ks[15], (DEPTH, D_MODEL), 0.02),
        "g_post_ffn": 1.0 + nrm(ks[16], (DEPTH, D_MODEL), 0.02),
        "w_up": nrm(ks[17], (DEPTH, D_MODEL, 2 * D_FF), D_MODEL ** -0.5),
        "w_conv_ffn": nrm(ks[18], (DEPTH, CONV_K, 2 * D_FF), CONV_K ** -0.5),
        "w_down": nrm(ks[19], (DEPTH, D_FF, D_MODEL), D_FF ** -0.5),
    }


def reference(x, c, positions, w_ada, b_ada, g_pre_mix, g_post_mix, w_in, lambda_q1, lambda_k1,
              lambda_q2, lambda_k2, g_subln, w_conv_mix, w_out, g_pre_ffn, g_post_ffn, w_up,
              w_conv_ffn, w_down):
    b, s, _ = x.shape
    cos, sin = rope_tables(positions, ATTN_HEAD_DIM)
    cond = jax.nn.silu(c)
    a_w, c_w = ATTN_WIDTH, CONV_WIDTH
    splits = [a_w, 2 * a_w, 3 * a_w, 3 * a_w + c_w, 3 * a_w + 2 * c_w]
    for l in range(DEPTH):
        lam_init = 0.8 - 0.6 * math.exp(-0.3 * l)
        mod = jnp.einsum('bd,de->be', cond, w_ada[l]) + b_ada[l]
        sh1, sc1, gt1, sh2, sc2, gt2 = [m[:, None, :] for m in jnp.split(mod, N_MOD, axis=-1)]

        h = rmsnorm(x, g_pre_mix[l]) * (1.0 + sc1) + sh1
        proj = jnp.einsum('bsd,de->bse', h, w_in[l])
        q, k, v, gate_b, gate_c, hc = jnp.split(proj, splits, axis=-1)

        q = apply_rope(q.reshape(b, s, ATTN_HEADS, 2, ATTN_HEAD_DIM), cos, sin)
        k = apply_rope(k.reshape(b, s, ATTN_HEADS, 2, ATTN_HEAD_DIM), cos, sin)
        v = v.reshape(b, s, ATTN_HEADS, 2 * ATTN_HEAD_DIM)
        lam = (jnp.exp(jnp.sum(lambda_q1[l].astype(jnp.float32) * lambda_k1[l].astype(jnp.float32)))
               - jnp.exp(jnp.sum(lambda_q2[l].astype(jnp.float32) * lambda_k2[l].astype(jnp.float32)))
               + lam_init)
        attn = diff_attention(q, k, v, lam)
        attn = (rmsnorm(attn, g_subln[l]) * (1.0 - lam_init)).reshape(b, s, a_w)

        conv = gate_b * causal_dwconv(gate_c * hc, w_conv_mix[l])

        mixed = jnp.concatenate([attn.astype(conv.dtype), conv], axis=-1)
        y = jnp.einsum('bse,ed->bsd', mixed, w_out[l])
        x = x + gt1 * rmsnorm(y, g_post_mix[l])

        h2 = rmsnorm(x, g_pre_ffn[l]) * (1.0 + sc2) + sh2
        u = causal_dwconv(jnp.einsum('bsd,df->bsf', h2, w_up[l]), w_conv_ffn[l])
        u_gate, u_val = jnp.split(u, 2, axis=-1)
        f = jnp.einsum('bsf,fd->bsd', jax.nn.silu(u_gate) * u_val, w_down[l])
        x = x + gt2 * rmsnorm(f, g_post_ffn[l])
    return x
```

```python
import functools
import math

import jax
import jax.numpy as jnp
from jax import lax
from jax.experimental import pallas as pl
from jax.experimental.pallas import tpu as pltpu

F32 = jnp.float32
BF16 = jnp.bfloat16

ATTN_HEADS = 8
ATTN_HEAD_DIM = 128
CONV_K = 3
ROPE_THETA = 10000.0
EPS = 1e-6
N_MOD = 6
LAM_INIT = 0.8 - 0.6 * math.exp(-0.3 * 0)

LANES = 128
SUBLANES = 8
BF16_ROWS = 16
VMEM_LIMIT = 56 * 1024 * 1024
ACC_COLS = 1024
EPILOGUE_ROWS = 128


def _params(*sem):
    return pltpu.CompilerParams(dimension_semantics=sem, vmem_limit_bytes=VMEM_LIMIT)


def _rms_scale(v):
    return lax.rsqrt(jnp.mean(v * v, axis=-1, keepdims=True) + EPS)


def _ada_kernel(c_ref, w_ref, b_ref, o_ref):
    c = c_ref[...]
    cond = c * jax.nn.sigmoid(c)
    o_ref[...] = jnp.dot(cond.astype(BF16), w_ref[...].astype(BF16),
                         preferred_element_type=F32) + b_ref[...]


def _ada_modulation(c, w_ada, b_ada, tn=512):
    b, d = c.shape
    n = w_ada.shape[1]
    rows = -(-b // SUBLANES) * SUBLANES
    c_pad = jnp.pad(c, ((0, rows - b), (0, 0)))
    out = pl.pallas_call(
        _ada_kernel,
        out_shape=jax.ShapeDtypeStruct((rows, n), F32),
        grid=(n // tn,),
        in_specs=[pl.BlockSpec((rows, d), lambda j: (0, 0)),
                  pl.BlockSpec((d, tn), lambda j: (0, j)),
                  pl.BlockSpec((1, tn), lambda j: (0, j))],
        out_specs=pl.BlockSpec((rows, tn), lambda j: (0, j)),
        compiler_params=_params("parallel"),
        name="ada_modulation",
    )(c_pad, w_ada, b_ada.reshape(1, n))
    return out[:b]


def _rope_kernel(pos_ref, inv_ref, cos_ref, sin_ref):
    ang = pos_ref[...].astype(F32) * inv_ref[...]
    lane = lax.broadcasted_iota(jnp.int32, ang.shape, 1)
    sin = jnp.sin(ang)
    cos_ref[...] = jnp.cos(ang)
    sin_ref[...] = jnp.where(lane < ATTN_HEAD_DIM // 2, -sin, sin)


def _rope_tables(positions, tm=2048):
    t = positions.size
    dim = ATTN_HEAD_DIM
    inv_freq = ROPE_THETA ** (-jnp.arange(0, dim, 2, dtype=F32) / dim)
    inv = jnp.concatenate([inv_freq, inv_freq]).reshape(1, dim)
    return pl.pallas_call(
        _rope_kernel,
        out_shape=(jax.ShapeDtypeStruct((t, dim), F32), jax.ShapeDtypeStruct((t, dim), F32)),
        grid=(t // tm,),
        in_specs=[pl.BlockSpec((tm, 1), lambda i: (i, 0)),
                  pl.BlockSpec((1, dim), lambda i: (0, 0))],
        out_specs=(pl.BlockSpec((tm, dim), lambda i: (i, 0)),
                   pl.BlockSpec((tm, dim), lambda i: (i, 0))),
        compiler_params=_params("parallel"),
        name="rope_tables",
    )(positions.reshape(t, 1), inv)


def _prenorm_kernel(x_ref, g_ref, sc_ref, sh_ref, o_ref):
    x = x_ref[...]
    y = x * _rms_scale(x) * g_ref[...]
    o_ref[...] = (y * (1.0 + sc_ref[0]) + sh_ref[0]).astype(o_ref.dtype)


def _prenorm(x2, g, sc, sh, seq, tm=512):
    t, d = x2.shape
    per_seq = seq // tm
    return pl.pallas_call(
        _prenorm_kernel,
        out_shape=jax.ShapeDtypeStruct((t, d), BF16),
        grid=(t // tm,),
        in_specs=[pl.BlockSpec((tm, d), lambda i: (i, 0)),
                  pl.BlockSpec((1, d), lambda i: (0, 0)),
                  pl.BlockSpec((1, 1, d), lambda i: (i // per_seq, 0, 0)),
                  pl.BlockSpec((1, 1, d), lambda i: (i // per_seq, 0, 0))],
        out_specs=pl.BlockSpec((tm, d), lambda i: (i, 0)),
        compiler_params=_params("parallel"),
        name="prenorm",
    )(x2, g.reshape(1, d), sc, sh)


def _inproj_kernel(h_ref, w_ref, cos_ref, sin_ref, o_ref, *, n_q, n_qk, q_scale):
    j = pl.program_id(1)
    acc = jnp.dot(h_ref[...], w_ref[...], preferred_element_type=F32)

    @pl.when(j >= n_qk)
    def _():
        o_ref[...] = acc.astype(o_ref.dtype)

    @pl.when(j < n_qk)
    def _():
        cos = cos_ref[...]
        sin = sin_ref[...]
        scale = jnp.where(j < n_q, q_scale, 1.0).astype(F32)
        for c in range(acc.shape[1] // LANES):
            t = acc[:, c * LANES:(c + 1) * LANES]
            r = (t * cos + pltpu.roll(t, ATTN_HEAD_DIM // 2, 1) * sin) * scale
            o_ref[:, c * LANES:(c + 1) * LANES] = r.astype(o_ref.dtype)


def _inproj(h, w_in, cos, sin, attn_width, tm=1024, tn=512):
    t, d = h.shape
    n = w_in.shape[1]
    kern = functools.partial(_inproj_kernel, n_q=attn_width // tn, n_qk=2 * attn_width // tn,
                             q_scale=ATTN_HEAD_DIM ** -0.5)
    return pl.pallas_call(
        kern,
        out_shape=jax.ShapeDtypeStruct((t, n), BF16),
        grid=(t // tm, n // tn),
        in_specs=[pl.BlockSpec((tm, d), lambda i, j: (i, 0)),
                  pl.BlockSpec((d, tn), lambda i, j: (0, j)),
                  pl.BlockSpec((tm, ATTN_HEAD_DIM), lambda i, j: (i, 0)),
                  pl.BlockSpec((tm, ATTN_HEAD_DIM), lambda i, j: (i, 0))],
        out_specs=pl.BlockSpec((tm, tn), lambda i, j: (i, j)),
        compiler_params=_params("parallel", "parallel"),
        name="inproj_rope",
    )(h, w_in, cos, sin)


def _attn_kernel(qi_tab, ki_tab, q_ref, k_ref, v_ref, lq1_ref, lk1_ref, lq2_ref, lk2_ref, g_ref,
                 o_ref, m_ref, l_ref, acc_ref):
    step = pl.program_id(2)
    qi = qi_tab[step]
    ki = ki_tab[step]
    d = ATTN_HEAD_DIM

    @pl.when(ki == 0)
    def _():
        m_ref[...] = jnp.full(m_ref.shape, -jnp.inf, F32)
        l_ref[...] = jnp.zeros(l_ref.shape, F32)
        acc_ref[...] = jnp.zeros(acc_ref.shape, F32)

    def update(masked):
        q = q_ref[...]
        k = k_ref[...]
        v = v_ref[...]
        for c in range(2):
            s = lax.dot_general(q[:, c * d:(c + 1) * d], k[:, c * d:(c + 1) * d],
                                (((1,), (1,)), ((), ())), preferred_element_type=F32)
            if masked:
                row = lax.broadcasted_iota(jnp.int32, s.shape, 0)
                col = lax.broadcasted_iota(jnp.int32, s.shape, 1)
                s = jnp.where(col <= row, s, jnp.finfo(F32).min)
            m_prev = m_ref[c]
            m_new = jnp.maximum(m_prev, jnp.max(s, axis=-1, keepdims=True))
            alpha = jnp.exp(m_prev - m_new)
            p = jnp.exp(s - m_new)
            l_ref[c] = alpha * l_ref[c] + jnp.sum(p, axis=-1, keepdims=True)
            acc_ref[c] = alpha * acc_ref[c] + jnp.dot(p.astype(BF16), v, preferred_element_type=F32)
            m_ref[c] = m_new

    @pl.when(ki < qi)
    def _():
        update(False)

    @pl.when(ki == qi)
    def _():
        update(True)
        lam = (jnp.exp(jnp.sum(lq1_ref[...] * lk1_ref[...], axis=-1, keepdims=True))
               - jnp.exp(jnp.sum(lq2_ref[...] * lk2_ref[...], axis=-1, keepdims=True)) + LAM_INIT)
        o = acc_ref[0] / l_ref[0] - lam * (acc_ref[1] / l_ref[1])
        o = o * _rms_scale(o) * g_ref[...] * (1.0 - LAM_INIT)
        o_ref[...] = o.astype(o_ref.dtype)


def _diff_attention(proj, lq1, lk1, lq2, lk2, g_subln, batch, seq, tq=512):
    t = proj.shape[0]
    hw = 2 * ATTN_HEAD_DIM
    nq = seq // tq
    pairs = [(qi, ki) for qi in range(nq) for ki in range(qi + 1)]
    qi_tab = jnp.asarray([p[0] for p in pairs], jnp.int32)
    ki_tab = jnp.asarray([p[1] for p in pairs], jnp.int32)
    k_off = ATTN_HEADS
    v_off = 2 * ATTN_HEADS
    vec = lambda a: a.reshape(1, -1)
    const = lambda b, h, s, qt, kt: (0, 0)
    grid_spec = pltpu.PrefetchScalarGridSpec(
        num_scalar_prefetch=2,
        grid=(batch, ATTN_HEADS, len(pairs)),
        in_specs=[pl.BlockSpec((tq, hw), lambda b, h, s, qt, kt: (b * nq + qt[s], h)),
                  pl.BlockSpec((tq, hw), lambda b, h, s, qt, kt: (b * nq + kt[s], k_off + h)),
                  pl.BlockSpec((tq, hw), lambda b, h, s, qt, kt: (b * nq + kt[s], v_off + h)),
                  pl.BlockSpec((1, ATTN_HEAD_DIM), const),
                  pl.BlockSpec((1, ATTN_HEAD_DIM), const),
                  pl.BlockSpec((1, ATTN_HEAD_DIM), const),
                  pl.BlockSpec((1, ATTN_HEAD_DIM), const),
                  pl.BlockSpec((1, hw), const)],
        out_specs=pl.BlockSpec((tq, hw), lambda b, h, s, qt, kt: (b * nq + qt[s], h)),
        scratch_shapes=[pltpu.VMEM((2, tq, 1), F32),
                        pltpu.VMEM((2, tq, 1), F32),
                        pltpu.VMEM((2, tq, hw), F32)],
    )
    return pl.pallas_call(
        _attn_kernel,
        out_shape=jax.ShapeDtypeStruct((t, ATTN_HEADS * hw), BF16),
        grid_spec=grid_spec,
        compiler_params=_params("parallel", "parallel", "arbitrary"),
        name="diff_attention",
    )(qi_tab, ki_tab, proj, proj, proj, vec(lq1), vec(lk1), vec(lq2), vec(lk2), vec(g_subln))


def _conv3(p, tail, w):
    w0, w1, w2 = w[0:1], w[1:2], w[2:3]
    body = pltpu.roll(p, 2, 0) * w0 + pltpu.roll(p, 1, 0) * w1 + p * w2
    first = p[0:SUBLANES]
    row = lax.broadcasted_iota(jnp.int32, first.shape, 0)
    h1 = jnp.where(row < 1, pltpu.roll(tail, 1, 0), pltpu.roll(first, 1, 0))
    h2 = jnp.where(row < 2, pltpu.roll(tail, 2, 0), pltpu.roll(first, 2, 0))
    head = h2 * w0 + h1 * w1 + first * w2
    return body, head


def _convmix_kernel(gb_ref, gc_ref, hc_ref, gct_ref, hct_ref, w_ref, o_ref, *, per_seq):
    i = pl.program_id(0)
    p = gc_ref[...].astype(F32) * hc_ref[...].astype(F32)
    tail = (gct_ref[...].astype(F32) * hct_ref[...].astype(F32))[BF16_ROWS - SUBLANES:]
    tail = jnp.where(i % per_seq == 0, 0.0, tail)
    body, head = _conv3(p, tail, w_ref[...])
    gb = gb_ref[...].astype(F32)
    o_ref[...] = (gb * body).astype(o_ref.dtype)
    o_ref[0:SUBLANES, :] = (gb[0:SUBLANES] * head).astype(o_ref.dtype)


def _conv_mixer(proj, w_conv, attn_width, conv_width, seq, tm=512, tc=512):
    t = proj.shape[0]
    per_seq = seq // tm
    cb = lambda off: (off // tc)
    b_off, c_off, h_off = cb(3 * attn_width), cb(3 * attn_width + conv_width), cb(3 * attn_width + 2 * conv_width)
    tail_blocks = tm // BF16_ROWS
    tail_idx = lambda i: jnp.maximum(i * tail_blocks - 1, 0)
    return pl.pallas_call(
        functools.partial(_convmix_kernel, per_seq=per_seq),
        out_shape=jax.ShapeDtypeStruct((t, conv_width), BF16),
        grid=(t // tm, conv_width // tc),
        in_specs=[pl.BlockSpec((tm, tc), lambda i, j: (i, b_off + j)),
                  pl.BlockSpec((tm, tc), lambda i, j: (i, c_off + j)),
                  pl.BlockSpec((tm, tc), lambda i, j: (i, h_off + j)),
                  pl.BlockSpec((BF16_ROWS, tc), lambda i, j: (tail_idx(i), c_off + j)),
                  pl.BlockSpec((BF16_ROWS, tc), lambda i, j: (tail_idx(i), h_off + j)),
                  pl.BlockSpec((CONV_K, tc), lambda i, j: (0, j))],
        out_specs=pl.BlockSpec((tm, tc), lambda i, j: (i, j)),
        compiler_params=_params("parallel", "parallel"),
        name="conv_mixer",
    )(proj, proj, proj, proj, proj, w_conv)


def _accumulate_dot(acc_ref, a, w_ref, first, k_rows=None):
    n = acc_ref.shape[1]
    for c in range(n // ACC_COLS):
        cols = slice(c * ACC_COLS, (c + 1) * ACC_COLS)
        w = w_ref[:, cols] if k_rows is None else w_ref[0:k_rows, cols]
        part = jnp.dot(a, w, preferred_element_type=F32)
        if first:
            acc_ref[:, cols] = part
        else:
            acc_ref[:, cols] += part


def _for_row_chunks(rows, body):
    def step(r, carry):
        body(pl.ds(pl.multiple_of(r * EPILOGUE_ROWS, EPILOGUE_ROWS), EPILOGUE_ROWS))
        return carry
    lax.fori_loop(0, rows // EPILOGUE_ROWS, step, 0)


def _outproj_kernel(a_ref, c_ref, w_ref, x_ref, gt_ref, gpost_ref, gpre_ref, sc_ref, sh_ref,
                    x1_ref, h2_ref, *, nk_attn):
    k = pl.program_id(1)

    @pl.when(k == 0)
    def _():
        _accumulate_dot(x1_ref, a_ref[...], w_ref, first=True)

    @pl.when((k > 0) & (k < nk_attn))
    def _():
        _accumulate_dot(x1_ref, a_ref[...], w_ref, first=False)

    @pl.when(k >= nk_attn)
    def _():
        _accumulate_dot(x1_ref, c_ref[...], w_ref, first=False)

    @pl.when(k == pl.num_programs(1) - 1)
    def _():
        def rows(rs):
            y = x1_ref[rs, :]
            x1 = x_ref[rs, :] + gt_ref[0] * (y * _rms_scale(y) * gpost_ref[...])
            x1_ref[rs, :] = x1
            h2 = x1 * _rms_scale(x1) * gpre_ref[...]
            h2_ref[rs, :] = (h2 * (1.0 + sc_ref[0]) + sh_ref[0]).astype(h2_ref.dtype)
        _for_row_chunks(x1_ref.shape[0], rows)


def _outproj(attn, conv, w_out, x2, gt, g_post, g_pre, sc, sh, seq, tm=512, tk=512):
    t, d = x2.shape
    ka = attn.shape[1]
    nk_attn = ka // tk
    nk = d // tk
    assert nk > nk_attn >= 1
    per_seq = seq // tm
    row = lambda i, k: (0, 0)
    mod = lambda i, k: (i // per_seq, 0, 0)
    return pl.pallas_call(
        functools.partial(_outproj_kernel, nk_attn=nk_attn),
        out_shape=(jax.ShapeDtypeStruct((t, d), F32), jax.ShapeDtypeStruct((t, d), BF16)),
        grid=(t // tm, nk),
        in_specs=[pl.BlockSpec((tm, tk), lambda i, k: (i, jnp.minimum(k, nk_attn - 1))),
                  pl.BlockSpec((tm, tk), lambda i, k: (i, jnp.maximum(k - nk_attn, 0))),
                  pl.BlockSpec((tk, d), lambda i, k: (k, 0)),
                  pl.BlockSpec((tm, d), lambda i, k: (i, 0)),
                  pl.BlockSpec((1, 1, d), mod),
                  pl.BlockSpec((1, d), row),
                  pl.BlockSpec((1, d), row),
                  pl.BlockSpec((1, 1, d), mod),
                  pl.BlockSpec((1, 1, d), mod)],
        out_specs=(pl.BlockSpec((tm, d), lambda i, k: (i, 0)),
                   pl.BlockSpec((tm, d), lambda i, k: (i, 0))),
        compiler_params=_params("parallel", "arbitrary"),
        name="outproj_norms",
    )(attn, conv, w_out, x2, gt, g_post.reshape(1, d), g_pre.reshape(1, d), sc, sh)


def _upproj_kernel(h_ref, wg_ref, wv_ref, cg_ref, cv_ref, o_ref, tail_g, tail_v, *, per_seq):
    i = pl.program_id(0)
    j = pl.program_id(1)
    seq_start = i % per_seq == 0
    h = h_ref[...]
    ug = jnp.dot(h, wg_ref[...], preferred_element_type=F32)
    uv = jnp.dot(h, wv_ref[...], preferred_element_type=F32)
    tg = jnp.where(seq_start, 0.0, tail_g[j])
    tv = jnp.where(seq_start, 0.0, tail_v[j])
    rows = ug.shape[0]
    tail_g[j] = ug[rows - SUBLANES:]
    tail_v[j] = uv[rows - SUBLANES:]
    g_body, g_head = _conv3(ug, tg, cg_ref[...])
    v_body, v_head = _conv3(uv, tv, cv_ref[...])
    o_ref[...] = (g_body * jax.nn.sigmoid(g_body) * v_body).astype(o_ref.dtype)
    o_ref[0:SUBLANES, :] = (g_head * jax.nn.sigmoid(g_head) * v_head).astype(o_ref.dtype)


def _upproj(h2, w_up, w_conv, d_ff, seq, tm=1024, tn=256):
    t, d = h2.shape
    nj = d_ff // tn
    per_seq = seq // tm
    return pl.pallas_call(
        functools.partial(_upproj_kernel, per_seq=per_seq),
        out_shape=jax.ShapeDtypeStruct((t, d_ff), BF16),
        grid=(t // tm, nj),
        in_specs=[pl.BlockSpec((tm, d), lambda i, j: (i, 0)),
                  pl.BlockSpec((d, tn), lambda i, j: (0, j)),
                  pl.BlockSpec((d, tn), lambda i, j: (0, nj + j)),
                  pl.BlockSpec((CONV_K, tn), lambda i, j: (0, j)),
                  pl.BlockSpec((CONV_K, tn), lambda i, j: (0, nj + j))],
        out_specs=pl.BlockSpec((tm, tn), lambda i, j: (i, j)),
        scratch_shapes=[pltpu.VMEM((nj, SUBLANES, tn), F32),
                        pltpu.VMEM((nj, SUBLANES, tn), F32)],
        compiler_params=_params("arbitrary", "arbitrary"),
        name="upproj_conv_gate",
    )(h2, w_up, w_up, w_conv, w_conv)


def _downproj_kernel(g_ref, w_ref, x1_ref, gt_ref, gpost_ref, o_ref, *, nk, k_last):
    k = pl.program_id(1)

    @pl.when(k == 0)
    def _():
        _accumulate_dot(o_ref, g_ref[...], w_ref, first=True)

    @pl.when((k > 0) & (k < nk - 1))
    def _():
        _accumulate_dot(o_ref, g_ref[...], w_ref, first=False)

    @pl.when(k == nk - 1)
    def _():
        _accumulate_dot(o_ref, g_ref[:, 0:k_last], w_ref, first=False, k_rows=k_last)

        def rows(rs):
            f = o_ref[rs, :]
            o_ref[rs, :] = x1_ref[rs, :] + gt_ref[0] * (f * _rms_scale(f) * gpost_ref[...])
        _for_row_chunks(o_ref.shape[0], rows)


def _downproj(g, w_down, x1, gt, g_post, seq, tm=512, tk=1024):
    t, d = x1.shape
    d_ff = g.shape[1]
    nk = pl.cdiv(d_ff, tk)
    k_last = d_ff - (nk - 1) * tk
    assert nk >= 3
    per_seq = seq // tm
    return pl.pallas_call(
        functools.partial(_downproj_kernel, nk=nk, k_last=k_last),
        out_shape=jax.ShapeDtypeStruct((t, d), F32),
        grid=(t // tm, nk),
        in_specs=[pl.BlockSpec((tm, tk), lambda i, k: (i, k)),
                  pl.BlockSpec((tk, d), lambda i, k: (k, 0)),
                  pl.BlockSpec((tm, d), lambda i, k: (i, 0)),
                  pl.BlockSpec((1, 1, d), lambda i, k: (i // per_seq, 0, 0)),
                  pl.BlockSpec((1, d), lambda i, k: (0, 0))],
        out_specs=pl.BlockSpec((tm, d), lambda i, k: (i, 0)),
        compiler_params=_params("parallel", "arbitrary"),
        name="downproj_norm",
    )(g, w_down, x1, gt, g_post.reshape(1, d))


def kernel(x, c, positions, w_ada, b_ada, g_pre_mix, g_post_mix, w_in, lambda_q1, lambda_k1, lambda_q2, lambda_k2, g_subln, w_conv_mix, w_out, g_pre_ffn, g_post_ffn, w_up, w_conv_ffn, w_down):
    batch, seq, d = x.shape
    depth = w_ada.shape[0]
    assert depth == 1, "the attention lambda schedule below is written for a single layer"
    attn_width = ATTN_HEADS * 2 * ATTN_HEAD_DIM
    conv_width = d - attn_width
    d_ff = w_down.shape[1]
    t = batch * seq

    cos, sin = _rope_tables(positions)
    x2 = x.reshape(t, d)
    for l in range(depth):
        mod = _ada_modulation(c, w_ada[l], b_ada[l])
        sh1, sc1, gt1, sh2, sc2, gt2 = [m.reshape(batch, 1, d) for m in jnp.split(mod, N_MOD, axis=-1)]

        h = _prenorm(x2, g_pre_mix[l], sc1, sh1, seq)
        proj = _inproj(h, w_in[l].astype(BF16), cos, sin, attn_width)
        attn = _diff_attention(proj, lambda_q1[l], lambda_k1[l], lambda_q2[l], lambda_k2[l],
                               g_subln[l], batch, seq)
        conv = _conv_mixer(proj, w_conv_mix[l], attn_width, conv_width, seq)
        x1, h2 = _outproj(attn, conv, w_out[l].astype(BF16), x2, gt1, g_post_mix[l], g_pre_ffn[l],
                          sc2, sh2, seq)
        gated = _upproj(h2, w_up[l].astype(BF16), w_conv_ffn[l], d_ff, seq)
        x2 = _downproj(gated, w_down[l].astype(BF16), x1, gt2, g_post_ffn[l], seq)
    return x2.reshape(batch, seq, d)
```

```python
import functools
import math

import jax
import jax.numpy as jnp
from jax import lax
from jax.experimental import pallas as pl
from jax.experimental.pallas import tpu as pltpu

F32 = jnp.float32
BF16 = jnp.bfloat16

ATTN_HEADS = 8
ATTN_HEAD_DIM = 128
CONV_K = 3
ROPE_THETA = 10000.0
EPS = 1e-6
N_MOD = 6
LAM_INIT = 0.8 - 0.6 * math.exp(-0.3 * 0)

LANES = 128
SUBLANES = 8
BF16_ROWS = 16
VMEM_LIMIT = 56 * 1024 * 1024
ACC_COLS = 1024
EPILOGUE_ROWS = 128
DOT_ROWS = 256
ATTN_Q_COLS = 256


def _params(*sem):
    return pltpu.CompilerParams(dimension_semantics=sem, vmem_limit_bytes=VMEM_LIMIT)


def _rms_scale(v):
    return lax.rsqrt(jnp.mean(v * v, axis=-1, keepdims=True) + EPS)


def _ada_kernel(c_ref, w_ref, b_ref, o_ref):
    c = c_ref[...]
    cond = c * jax.nn.sigmoid(c)
    o_ref[...] = jnp.dot(cond.astype(BF16), w_ref[...].astype(BF16),
                         preferred_element_type=F32) + b_ref[...]


def _ada_modulation(c, w_ada, b_ada, tn=512):
    b, d = c.shape
    n = w_ada.shape[1]
    rows = -(-b // SUBLANES) * SUBLANES
    c_pad = jnp.pad(c, ((0, rows - b), (0, 0)))
    out = pl.pallas_call(
        _ada_kernel,
        out_shape=jax.ShapeDtypeStruct((rows, n), F32),
        grid=(n // tn,),
        in_specs=[pl.BlockSpec((rows, d), lambda j: (0, 0)),
                  pl.BlockSpec((d, tn), lambda j: (0, j)),
                  pl.BlockSpec((1, tn), lambda j: (0, j))],
        out_specs=pl.BlockSpec((rows, tn), lambda j: (0, j)),
        compiler_params=_params("parallel"),
        name="ada_modulation",
    )(c_pad, w_ada, b_ada.reshape(1, n))
    return out[:b]


def _rope_kernel(pos_ref, inv_ref, cos_ref, sin_ref):
    ang = pos_ref[...].astype(F32) * inv_ref[...]
    lane = lax.broadcasted_iota(jnp.int32, ang.shape, 1)
    sin = jnp.sin(ang)
    cos_ref[...] = jnp.cos(ang)
    sin_ref[...] = jnp.where(lane < ATTN_HEAD_DIM // 2, -sin, sin)


def _rope_tables(positions, tm=2048):
    t = positions.size
    dim = ATTN_HEAD_DIM
    inv_freq = ROPE_THETA ** (-jnp.arange(0, dim, 2, dtype=F32) / dim)
    inv = jnp.concatenate([inv_freq, inv_freq]).reshape(1, dim)
    return pl.pallas_call(
        _rope_kernel,
        out_shape=(jax.ShapeDtypeStruct((t, dim), F32), jax.ShapeDtypeStruct((t, dim), F32)),
        grid=(t // tm,),
        in_specs=[pl.BlockSpec((tm, 1), lambda i: (i, 0)),
                  pl.BlockSpec((1, dim), lambda i: (0, 0))],
        out_specs=(pl.BlockSpec((tm, dim), lambda i: (i, 0)),
                   pl.BlockSpec((tm, dim), lambda i: (i, 0))),
        compiler_params=_params("parallel"),
        name="rope_tables",
    )(positions.reshape(t, 1), inv)


def _prenorm_kernel(x_ref, g_ref, sc_ref, sh_ref, o_ref):
    x = x_ref[...]
    y = x * _rms_scale(x) * g_ref[...]
    o_ref[...] = (y * (1.0 + sc_ref[0]) + sh_ref[0]).astype(o_ref.dtype)


def _prenorm(x2, g, sc, sh, seq, tm=512):
    t, d = x2.shape
    per_seq = seq // tm
    return pl.pallas_call(
        _prenorm_kernel,
        out_shape=jax.ShapeDtypeStruct((t, d), BF16),
        grid=(t // tm,),
        in_specs=[pl.BlockSpec((tm, d), lambda i: (i, 0)),
                  pl.BlockSpec((1, d), lambda i: (0, 0)),
                  pl.BlockSpec((1, 1, d), lambda i: (i // per_seq, 0, 0)),
                  pl.BlockSpec((1, 1, d), lambda i: (i // per_seq, 0, 0))],
        out_specs=pl.BlockSpec((tm, d), lambda i: (i, 0)),
        compiler_params=_params("parallel"),
        name="prenorm",
    )(x2, g.reshape(1, d), sc, sh)


def _inproj_kernel(h_ref, w_ref, cos_ref, sin_ref, o_ref, *, n_q, n_qk, q_scale):
    j = pl.program_id(1)
    tm, tn = o_ref.shape

    @pl.when(j >= n_qk)
    def _():
        for r in range(tm // DOT_ROWS):
            rows = slice(r * DOT_ROWS, (r + 1) * DOT_ROWS)
            acc = jnp.dot(h_ref[rows, :], w_ref[...], preferred_element_type=F32)
            o_ref[rows, :] = acc.astype(o_ref.dtype)

    @pl.when(j < n_qk)
    def _():
        scale = jnp.where(j < n_q, q_scale, 1.0).astype(F32)
        for r in range(tm // DOT_ROWS):
            rows = slice(r * DOT_ROWS, (r + 1) * DOT_ROWS)
            acc = jnp.dot(h_ref[rows, :], w_ref[...], preferred_element_type=F32)
            cos = cos_ref[rows, :]
            sin = sin_ref[rows, :] * scale
            cos = cos * scale
            for c in range(tn // LANES):
                t = acc[:, c * LANES:(c + 1) * LANES]
                rot = t * cos + pltpu.roll(t, ATTN_HEAD_DIM // 2, 1) * sin
                o_ref[rows, c * LANES:(c + 1) * LANES] = rot.astype(o_ref.dtype)


def _inproj(h, w_in, cos, sin, attn_width, tm=1024, tn=1024):
    t, d = h.shape
    n = w_in.shape[1]
    kern = functools.partial(_inproj_kernel, n_q=attn_width // tn, n_qk=2 * attn_width // tn,
                             q_scale=ATTN_HEAD_DIM ** -0.5 * math.log2(math.e))
    return pl.pallas_call(
        kern,
        out_shape=jax.ShapeDtypeStruct((t, n), BF16),
        grid=(t // tm, n // tn),
        in_specs=[pl.BlockSpec((tm, d), lambda i, j: (i, 0)),
                  pl.BlockSpec((d, tn), lambda i, j: (0, j)),
                  pl.BlockSpec((tm, ATTN_HEAD_DIM), lambda i, j: (i, 0)),
                  pl.BlockSpec((tm, ATTN_HEAD_DIM), lambda i, j: (i, 0))],
        out_specs=pl.BlockSpec((tm, tn), lambda i, j: (i, j)),
        compiler_params=_params("parallel", "parallel"),
        name="inproj_rope",
    )(h, w_in, cos, sin)


def _attn_kernel(q_ref, k_ref, v_ref, lq1_ref, lk1_ref, lq2_ref, lk2_ref, g_ref,
                 o_ref, vt_ref, acc_ref):
    qi = pl.program_id(2)
    tq = q_ref.shape[0]
    d = ATTN_HEAD_DIM

    diag_rows = pl.ds(pl.multiple_of(qi * tq, tq), tq)
    vt_ref[qi] = v_ref[diag_rows, :].T
    q_t = q_ref[...].T
    acc_ref[...] = jnp.zeros(acc_ref.shape, F32)

    chains = [(c, h) for c in range(2) for h in range(tq // ATTN_Q_COLS)]

    def score_dots(j):
        k = k_ref[pl.ds(pl.multiple_of(j * tq, tq), tq), :]
        out = []
        for c, h in chains:
            cols = slice(h * ATTN_Q_COLS, (h + 1) * ATTN_Q_COLS)
            out.append(jnp.dot(k[:, c * d:(c + 1) * d], q_t[c * d:(c + 1) * d, cols],
                               preferred_element_type=F32))
        return tuple(out)

    def chunk(j, scores, m, l, masked):
        v_t = vt_ref[j]
        m_out, l_out = [], []
        for n, (c, h) in enumerate(chains):
            cols = slice(h * ATTN_Q_COLS, (h + 1) * ATTN_Q_COLS)
            s = scores[n]
            if masked:
                kv = lax.broadcasted_iota(jnp.int32, s.shape, 0)
                qq = lax.broadcasted_iota(jnp.int32, s.shape, 1) + h * ATTN_Q_COLS
                s = jnp.where(kv <= qq, s, jnp.finfo(F32).min)
            m_new = jnp.maximum(m[n], jnp.max(s, axis=0, keepdims=True))
            alpha = jnp.exp2(m[n] - m_new)
            p = jnp.exp2(s - m_new)
            l_out.append(alpha * l[n] + jnp.sum(p, axis=0, keepdims=True))
            acc_ref[c, :, cols] = alpha * acc_ref[c, :, cols] + jnp.dot(
                v_t, p.astype(BF16), preferred_element_type=F32)
            m_out.append(m_new)
        return tuple(m_out), tuple(l_out)

    neg = jnp.full((1, ATTN_Q_COLS), -jnp.inf, F32)
    zero = jnp.zeros((1, ATTN_Q_COLS), F32)
    m, l = lax.fori_loop(0, qi, lambda j, ml: chunk(j, score_dots(j), ml[0], ml[1], False),
                         ((neg,) * len(chains), (zero,) * len(chains)))
    m, l = chunk(qi, score_dots(qi), m, l, True)
    l = [jnp.concatenate([l[n] for n, (c, h) in enumerate(chains) if c == cc], axis=1)
         for cc in range(2)]

    lam = (jnp.exp(jnp.sum(lq1_ref[...] * lk1_ref[...], axis=-1, keepdims=True))
           - jnp.exp(jnp.sum(lq2_ref[...] * lk2_ref[...], axis=-1, keepdims=True)) + LAM_INIT)
    o_t = acc_ref[0] * (1.0 / l[0]) - acc_ref[1] * (lam / l[1])
    o_t = o_t * lax.rsqrt(jnp.mean(o_t * o_t, axis=0, keepdims=True) + EPS)
    o_ref[...] = (o_t.T * (g_ref[...] * (1.0 - LAM_INIT))).astype(o_ref.dtype)


def _diff_attention(proj, lq1, lk1, lq2, lk2, g_subln, batch, seq, tq=512):
    t = proj.shape[0]
    hw = 2 * ATTN_HEAD_DIM
    nq = seq // tq
    k_off = ATTN_HEADS
    v_off = 2 * ATTN_HEADS
    vec = lambda a: a.reshape(1, -1)
    const = lambda b, h, i: (0, 0)
    return pl.pallas_call(
        _attn_kernel,
        out_shape=jax.ShapeDtypeStruct((t, ATTN_HEADS * hw), BF16),
        grid=(batch, ATTN_HEADS, nq),
        in_specs=[pl.BlockSpec((tq, hw), lambda b, h, i: (b * nq + i, h)),
                  pl.BlockSpec((seq, hw), lambda b, h, i: (b, k_off + h)),
                  pl.BlockSpec((seq, hw), lambda b, h, i: (b, v_off + h)),
                  pl.BlockSpec((1, ATTN_HEAD_DIM), const),
                  pl.BlockSpec((1, ATTN_HEAD_DIM), const),
                  pl.BlockSpec((1, ATTN_HEAD_DIM), const),
                  pl.BlockSpec((1, ATTN_HEAD_DIM), const),
                  pl.BlockSpec((1, hw), const)],
        out_specs=pl.BlockSpec((tq, hw), lambda b, h, i: (b * nq + i, h)),
        scratch_shapes=[pltpu.VMEM((nq, hw, tq), BF16),
                        pltpu.VMEM((2, hw, tq), F32)],
        compiler_params=_params("parallel", "parallel", "arbitrary"),
        name="diff_attention",
    )(proj, proj, proj, vec(lq1), vec(lk1), vec(lq2), vec(lk2), vec(g_subln))


def _conv3(p, tail, w):
    w0, w1, w2 = w[0:1], w[1:2], w[2:3]
    body = pltpu.roll(p, 2, 0) * w0 + pltpu.roll(p, 1, 0) * w1 + p * w2
    first = p[0:SUBLANES]
    row = lax.broadcasted_iota(jnp.int32, first.shape, 0)
    h1 = jnp.where(row < 1, pltpu.roll(tail, 1, 0), pltpu.roll(first, 1, 0))
    h2 = jnp.where(row < 2, pltpu.roll(tail, 2, 0), pltpu.roll(first, 2, 0))
    head = jnp.concatenate([h2 * w0 + h1 * w1 + first * w2, body[SUBLANES:BF16_ROWS]], axis=0)
    return body, head


def _convmix_kernel(gb_ref, gc_ref, hc_ref, gct_ref, hct_ref, w_ref, o_ref, *, per_seq):
    i = pl.program_id(0)
    p = gc_ref[...].astype(F32) * hc_ref[...].astype(F32)
    tail = (gct_ref[...].astype(F32) * hct_ref[...].astype(F32))[BF16_ROWS - SUBLANES:]
    tail = jnp.where(i % per_seq == 0, 0.0, tail)
    body, head = _conv3(p, tail, w_ref[...])
    gb = gb_ref[...].astype(F32)
    o_ref[...] = (gb * body).astype(o_ref.dtype)
    o_ref[0:BF16_ROWS, :] = (gb[0:BF16_ROWS] * head).astype(o_ref.dtype)


def _conv_mixer(proj, w_conv, attn_width, conv_width, seq, tm=512, tc=512):
    t = proj.shape[0]
    per_seq = seq // tm
    cb = lambda off: (off // tc)
    b_off, c_off, h_off = cb(3 * attn_width), cb(3 * attn_width + conv_width), cb(3 * attn_width + 2 * conv_width)
    tail_blocks = tm // BF16_ROWS
    tail_idx = lambda i: jnp.maximum(i * tail_blocks - 1, 0)
    return pl.pallas_call(
        functools.partial(_convmix_kernel, per_seq=per_seq),
        out_shape=jax.ShapeDtypeStruct((t, conv_width), BF16),
        grid=(t // tm, conv_width // tc),
        in_specs=[pl.BlockSpec((tm, tc), lambda i, j: (i, b_off + j)),
                  pl.BlockSpec((tm, tc), lambda i, j: (i, c_off + j)),
                  pl.BlockSpec((tm, tc), lambda i, j: (i, h_off + j)),
                  pl.BlockSpec((BF16_ROWS, tc), lambda i, j: (tail_idx(i), c_off + j)),
                  pl.BlockSpec((BF16_ROWS, tc), lambda i, j: (tail_idx(i), h_off + j)),
                  pl.BlockSpec((CONV_K, tc), lambda i, j: (0, j))],
        out_specs=pl.BlockSpec((tm, tc), lambda i, j: (i, j)),
        compiler_params=_params("parallel", "parallel"),
        name="conv_mixer",
    )(proj, proj, proj, proj, proj, w_conv)


def _accumulate_dot(acc_ref, a, w_ref, first, k_rows=None):
    n = acc_ref.shape[1]
    for c in range(n // ACC_COLS):
        cols = slice(c * ACC_COLS, (c + 1) * ACC_COLS)
        w = w_ref[:, cols] if k_rows is None else w_ref[0:k_rows, cols]
        part = jnp.dot(a, w, preferred_element_type=F32)
        if first:
            acc_ref[:, cols] = part
        else:
            acc_ref[:, cols] += part


def _for_row_chunks(rows, body):
    def step(r, carry):
        body(pl.ds(pl.multiple_of(r * EPILOGUE_ROWS, EPILOGUE_ROWS), EPILOGUE_ROWS))
        return carry
    lax.fori_loop(0, rows // EPILOGUE_ROWS, step, 0)


def _outproj_kernel(a_ref, c_ref, w_ref, x_ref, gt_ref, gpost_ref, gpre_ref, sc_ref, sh_ref,
                    x1_ref, h2_ref, *, nk_attn):
    k = pl.program_id(1)

    @pl.when(k == 0)
    def _():
        _accumulate_dot(x1_ref, a_ref[...], w_ref, first=True)

    @pl.when((k > 0) & (k < nk_attn))
    def _():
        _accumulate_dot(x1_ref, a_ref[...], w_ref, first=False)

    @pl.when(k >= nk_attn)
    def _():
        _accumulate_dot(x1_ref, c_ref[...], w_ref, first=False)

    @pl.when(k == pl.num_programs(1) - 1)
    def _():
        def rows(rs):
            y = x1_ref[rs, :]
            x1 = x_ref[rs, :] + gt_ref[0] * (y * _rms_scale(y) * gpost_ref[...])
            x1_ref[rs, :] = x1
            h2 = x1 * _rms_scale(x1) * gpre_ref[...]
            h2_ref[rs, :] = (h2 * (1.0 + sc_ref[0]) + sh_ref[0]).astype(h2_ref.dtype)
        _for_row_chunks(x1_ref.shape[0], rows)


def _outproj(attn, conv, w_out, x2, gt, g_post, g_pre, sc, sh, seq, tm=512, tk=512):
    t, d = x2.shape
    ka = attn.shape[1]
    nk_attn = ka // tk
    nk = d // tk
    assert nk > nk_attn >= 1
    per_seq = seq // tm
    row = lambda i, k: (0, 0)
    mod = lambda i, k: (i // per_seq, 0, 0)
    return pl.pallas_call(
        functools.partial(_outproj_kernel, nk_attn=nk_attn),
        out_shape=(jax.ShapeDtypeStruct((t, d), F32), jax.ShapeDtypeStruct((t, d), BF16)),
        grid=(t // tm, nk),
        in_specs=[pl.BlockSpec((tm, tk), lambda i, k: (i, jnp.minimum(k, nk_attn - 1))),
                  pl.BlockSpec((tm, tk), lambda i, k: (i, jnp.maximum(k - nk_attn, 0))),
                  pl.BlockSpec((tk, d), lambda i, k: (k, 0)),
                  pl.BlockSpec((tm, d), lambda i, k: (i, 0)),
                  pl.BlockSpec((1, 1, d), mod),
                  pl.BlockSpec((1, d), row),
                  pl.BlockSpec((1, d), row),
                  pl.BlockSpec((1, 1, d), mod),
                  pl.BlockSpec((1, 1, d), mod)],
        out_specs=(pl.BlockSpec((tm, d), lambda i, k: (i, 0)),
                   pl.BlockSpec((tm, d), lambda i, k: (i, 0))),
        compiler_params=_params("parallel", "arbitrary"),
        name="outproj_norms",
    )(attn, conv, w_out, x2, gt, g_post.reshape(1, d), g_pre.reshape(1, d), sc, sh)


def _upproj_kernel(h_ref, wg_ref, wv_ref, cg_ref, cv_ref, o_ref, tail_g, tail_v, *, per_seq):
    i = pl.program_id(0)
    j = pl.program_id(1)
    @pl.when(i % per_seq == 0)
    def _():
        tail_g[j] = jnp.zeros(tail_g.shape[1:], F32)
        tail_v[j] = jnp.zeros(tail_v.shape[1:], F32)

    tg = tail_g[j]
    tv = tail_v[j]
    cg = cg_ref[...]
    cv = cv_ref[...]
    for r in range(o_ref.shape[0] // DOT_ROWS):
        rows = slice(r * DOT_ROWS, (r + 1) * DOT_ROWS)
        h = h_ref[rows, :]
        ug = jnp.dot(h, wg_ref[...], preferred_element_type=F32)
        uv = jnp.dot(h, wv_ref[...], preferred_element_type=F32)
        g_body, g_head = _conv3(ug, tg, cg)
        v_body, v_head = _conv3(uv, tv, cv)
        tg = ug[DOT_ROWS - SUBLANES:]
        tv = uv[DOT_ROWS - SUBLANES:]
        o_ref[rows, :] = (g_body * jax.nn.sigmoid(g_body) * v_body).astype(o_ref.dtype)
        head = slice(r * DOT_ROWS, r * DOT_ROWS + BF16_ROWS)
        o_ref[head, :] = (g_head * jax.nn.sigmoid(g_head) * v_head).astype(o_ref.dtype)
    tail_g[j] = tg
    tail_v[j] = tv


def _upproj(h2, w_up, w_conv, d_ff, seq, tm=2048, tn=256):
    t, d = h2.shape
    nj = d_ff // tn
    per_seq = seq // tm
    return pl.pallas_call(
        functools.partial(_upproj_kernel, per_seq=per_seq),
        out_shape=jax.ShapeDtypeStruct((t, d_ff), BF16),
        grid=(t // tm, nj),
        in_specs=[pl.BlockSpec((tm, d), lambda i, j: (i, 0)),
                  pl.BlockSpec((d, tn), lambda i, j: (0, j)),
                  pl.BlockSpec((d, tn), lambda i, j: (0, nj + j)),
                  pl.BlockSpec((CONV_K, tn), lambda i, j: (0, j)),
                  pl.BlockSpec((CONV_K, tn), lambda i, j: (0, nj + j))],
        out_specs=pl.BlockSpec((tm, tn), lambda i, j: (i, j)),
        scratch_shapes=[pltpu.VMEM((nj, SUBLANES, tn), F32),
                        pltpu.VMEM((nj, SUBLANES, tn), F32)],
        compiler_params=_params("arbitrary", "arbitrary"),
        name="upproj_conv_gate",
    )(h2, w_up, w_up, w_conv, w_conv)


def _downproj_kernel(g_ref, w_ref, x1_ref, gt_ref, gpost_ref, o_ref, *, nk, k_last):
    k = pl.program_id(1)

    @pl.when(k == 0)
    def _():
        _accumulate_dot(o_ref, g_ref[...], w_ref, first=True)

    @pl.when((k > 0) & (k < nk - 1))
    def _():
        _accumulate_dot(o_ref, g_ref[...], w_ref, first=False)

    @pl.when(k == nk - 1)
    def _():
        _accumulate_dot(o_ref, g_ref[:, 0:k_last], w_ref, first=False, k_rows=k_last)

        def rows(rs):
            f = o_ref[rs, :]
            o_ref[rs, :] = x1_ref[rs, :] + gt_ref[0] * (f * _rms_scale(f) * gpost_ref[...])
        _for_row_chunks(o_ref.shape[0], rows)


def _downproj(g, w_down, x1, gt, g_post, seq, tm=512, tk=1024):
    t, d = x1.shape
    d_ff = g.shape[1]
    nk = pl.cdiv(d_ff, tk)
    k_last = d_ff - (nk - 1) * tk
    assert nk >= 3
    per_seq = seq // tm
    return pl.pallas_call(
        functools.partial(_downproj_kernel, nk=nk, k_last=k_last),
        out_shape=jax.ShapeDtypeStruct((t, d), F32),
        grid=(t // tm, nk),
        in_specs=[pl.BlockSpec((tm, tk), lambda i, k: (i, k)),
                  pl.BlockSpec((tk, d), lambda i, k: (k, 0)),
                  pl.BlockSpec((tm, d), lambda i, k: (i, 0)),
                  pl.BlockSpec((1, 1, d), lambda i, k: (i // per_seq, 0, 0)),
                  pl.BlockSpec((1, d), lambda i, k: (0, 0))],
        out_specs=pl.BlockSpec((tm, d), lambda i, k: (i, 0)),
        compiler_params=_params("parallel", "arbitrary"),
        name="downproj_norm",
    )(g, w_down, x1, gt, g_post.reshape(1, d))


def kernel(x, c, positions, w_ada, b_ada, g_pre_mix, g_post_mix, w_in, lambda_q1, lambda_k1, lambda_q2, lambda_k2, g_subln, w_conv_mix, w_out, g_pre_ffn, g_post_ffn, w_up, w_conv_ffn, w_down):
    batch, seq, d = x.shape
    depth = w_ada.shape[0]
    assert depth == 1, "the attention lambda schedule below is written for a single layer"
    attn_width = ATTN_HEADS * 2 * ATTN_HEAD_DIM
    conv_width = d - attn_width
    d_ff = w_down.shape[1]
    t = batch * seq

    cos, sin = _rope_tables(positions)
    x2 = x.reshape(t, d)
    for l in range(depth):
        mod = _ada_modulation(c, w_ada[l], b_ada[l])
        sh1, sc1, gt1, sh2, sc2, gt2 = [m.reshape(batch, 1, d) for m in jnp.split(mod, N_MOD, axis=-1)]

        h = _prenorm(x2, g_pre_mix[l], sc1, sh1, seq)
        proj = _inproj(h, w_in[l].astype(BF16), cos, sin, attn_width)
        attn = _diff_attention(proj, lambda_q1[l], lambda_k1[l], lambda_q2[l], lambda_k2[l],
                               g_subln[l], batch, seq)
        conv = _conv_mixer(proj, w_conv_mix[l], attn_width, conv_width, seq)
        x1, h2 = _outproj(attn, conv, w_out[l].astype(BF16), x2, gt1, g_post_mix[l], g_pre_ffn[l],
                          sc2, sh2, seq)
        gated = _upproj(h2, w_up[l].astype(BF16), w_conv_ffn[l], d_ff, seq)
        x2 = _downproj(gated, w_down[l].astype(BF16), x1, gt2, g_post_ffn[l], seq)
    return x2.reshape(batch, seq, d)
```

```python
import functools
import math

import jax
import jax.numpy as jnp
from jax import lax
from jax.experimental import pallas as pl
from jax.experimental.pallas import tpu as pltpu

F32 = jnp.float32
BF16 = jnp.bfloat16

ATTN_HEADS = 8
ATTN_HEAD_DIM = 128
CONV_K = 3
ROPE_THETA = 10000.0
EPS = 1e-6
N_MOD = 6
LAM_INIT = 0.8 - 0.6 * math.exp(-0.3 * 0)

LANES = 128
SUBLANES = 8
BF16_ROWS = 16
VMEM_LIMIT = 56 * 1024 * 1024
ACC_COLS = 1024
EPILOGUE_ROWS = 128
DOT_ROWS = 256
ATTN_Q_COLS = 256


def _params(*sem):
    return pltpu.CompilerParams(dimension_semantics=sem, vmem_limit_bytes=VMEM_LIMIT)


def _rms_scale(v):
    return lax.rsqrt(jnp.mean(v * v, axis=-1, keepdims=True) + EPS)


def _ada_kernel(c_ref, w_ref, b_ref, o_ref):
    c = c_ref[...]
    cond = c * jax.nn.sigmoid(c)
    o_ref[...] = jnp.dot(cond.astype(BF16), w_ref[...].astype(BF16),
                         preferred_element_type=F32) + b_ref[...]


def _ada_modulation(c, w_ada, b_ada, tn=512):
    b, d = c.shape
    n = w_ada.shape[1]
    rows = -(-b // SUBLANES) * SUBLANES
    c_pad = jnp.pad(c, ((0, rows - b), (0, 0)))
    out = pl.pallas_call(
        _ada_kernel,
        out_shape=jax.ShapeDtypeStruct((rows, n), F32),
        grid=(n // tn,),
        in_specs=[pl.BlockSpec((rows, d), lambda j: (0, 0)),
                  pl.BlockSpec((d, tn), lambda j: (0, j)),
                  pl.BlockSpec((1, tn), lambda j: (0, j))],
        out_specs=pl.BlockSpec((rows, tn), lambda j: (0, j)),
        compiler_params=_params("parallel"),
        name="ada_modulation",
    )(c_pad, w_ada, b_ada.reshape(1, n))
    return out[:b]


def _rope_kernel(pos_ref, inv_ref, cos_ref, sin_ref):
    ang = pos_ref[...].astype(F32) * inv_ref[...]
    lane = lax.broadcasted_iota(jnp.int32, ang.shape, 1)
    sin = jnp.sin(ang)
    cos_ref[...] = jnp.cos(ang)
    sin_ref[...] = jnp.where(lane < ATTN_HEAD_DIM // 2, -sin, sin)


def _rope_tables(positions, tm=2048):
    t = positions.size
    dim = ATTN_HEAD_DIM
    inv_freq = ROPE_THETA ** (-jnp.arange(0, dim, 2, dtype=F32) / dim)
    inv = jnp.concatenate([inv_freq, inv_freq]).reshape(1, dim)
    return pl.pallas_call(
        _rope_kernel,
        out_shape=(jax.ShapeDtypeStruct((t, dim), F32), jax.ShapeDtypeStruct((t, dim), F32)),
        grid=(t // tm,),
        in_specs=[pl.BlockSpec((tm, 1), lambda i: (i, 0)),
                  pl.BlockSpec((1, dim), lambda i: (0, 0))],
        out_specs=(pl.BlockSpec((tm, dim), lambda i: (i, 0)),
                   pl.BlockSpec((tm, dim), lambda i: (i, 0))),
        compiler_params=_params("parallel"),
        name="rope_tables",
    )(positions.reshape(t, 1), inv)


def _prenorm_kernel(x_ref, g_ref, sc_ref, sh_ref, o_ref):
    x = x_ref[...]
    y = x * _rms_scale(x) * g_ref[...]
    o_ref[...] = (y * (1.0 + sc_ref[0]) + sh_ref[0]).astype(o_ref.dtype)


def _prenorm(x2, g, sc, sh, seq, tm=512):
    t, d = x2.shape
    per_seq = seq // tm
    return pl.pallas_call(
        _prenorm_kernel,
        out_shape=jax.ShapeDtypeStruct((t, d), BF16),
        grid=(t // tm,),
        in_specs=[pl.BlockSpec((tm, d), lambda i: (i, 0)),
                  pl.BlockSpec((1, d), lambda i: (0, 0)),
                  pl.BlockSpec((1, 1, d), lambda i: (i // per_seq, 0, 0)),
                  pl.BlockSpec((1, 1, d), lambda i: (i // per_seq, 0, 0))],
        out_specs=pl.BlockSpec((tm, d), lambda i: (i, 0)),
        compiler_params=_params("parallel"),
        name="prenorm",
    )(x2, g.reshape(1, d), sc, sh)


def _inproj_kernel(h_ref, w32_ref, cos_ref, sin_ref, o_ref, w_ref, *, n_q, n_qk, q_scale):
    j = pl.program_id(1)
    tm, tn = o_ref.shape
    w_ref[...] = w32_ref[...].astype(BF16)

    @pl.when(j >= n_qk)
    def _():
        for r in range(tm // DOT_ROWS):
            rows = slice(r * DOT_ROWS, (r + 1) * DOT_ROWS)
            acc = jnp.dot(h_ref[rows, :], w_ref[...], preferred_element_type=F32)
            o_ref[rows, :] = acc.astype(o_ref.dtype)

    @pl.when(j < n_qk)
    def _():
        scale = jnp.where(j < n_q, q_scale, 1.0).astype(F32)
        for r in range(tm // DOT_ROWS):
            rows = slice(r * DOT_ROWS, (r + 1) * DOT_ROWS)
            acc = jnp.dot(h_ref[rows, :], w_ref[...], preferred_element_type=F32)
            cos = cos_ref[rows, :]
            sin = sin_ref[rows, :] * scale
            cos = cos * scale
            for c in range(tn // LANES):
                t = acc[:, c * LANES:(c + 1) * LANES]
                rot = t * cos + pltpu.roll(t, ATTN_HEAD_DIM // 2, 1) * sin
                o_ref[rows, c * LANES:(c + 1) * LANES] = rot.astype(o_ref.dtype)


def _inproj(h, w_in, cos, sin, attn_width, tm=2048, tn=512):
    t, d = h.shape
    n = w_in.shape[1]
    kern = functools.partial(_inproj_kernel, n_q=attn_width // tn, n_qk=2 * attn_width // tn,
                             q_scale=ATTN_HEAD_DIM ** -0.5 * math.log2(math.e))
    return pl.pallas_call(
        kern,
        out_shape=jax.ShapeDtypeStruct((t, n), BF16),
        grid=(t // tm, n // tn),
        in_specs=[pl.BlockSpec((tm, d), lambda i, j: (i, 0), pipeline_mode=pl.Buffered(1)),
                  pl.BlockSpec((d, tn), lambda i, j: (0, j)),
                  pl.BlockSpec((tm, ATTN_HEAD_DIM), lambda i, j: (i, 0)),
                  pl.BlockSpec((tm, ATTN_HEAD_DIM), lambda i, j: (i, 0))],
        out_specs=pl.BlockSpec((tm, tn), lambda i, j: (i, j)),
        scratch_shapes=[pltpu.VMEM((d, tn), BF16)],
        compiler_params=_params("parallel", "parallel"),
        name="inproj_rope",
    )(h, w_in, cos, sin)


def _attn_kernel(q_ref, k_ref, v_ref, lq1_ref, lk1_ref, lq2_ref, lk2_ref, g_ref,
                 o_ref, vt_ref, acc_ref):
    qi = pl.program_id(2)
    tq = q_ref.shape[0]
    tk = vt_ref.shape[2]
    d = ATTN_HEAD_DIM
    diag_chunks = tq // tk
    first_diag = qi * diag_chunks

    for t in range(diag_chunks):
        rows = pl.ds(pl.multiple_of((first_diag + t) * tk, tk), tk)
        vt_ref[first_diag + t] = v_ref[rows, :].T
    q_t = q_ref[...].T
    acc_ref[...] = jnp.zeros(acc_ref.shape, F32)

    chains = [(c, h) for c in range(2) for h in range(tq // ATTN_Q_COLS)]

    def chunk(j, m, l, diag=None):
        k = k_ref[pl.ds(pl.multiple_of(j * tk, tk), tk), :]
        v_t = vt_ref[j]
        if diag is None:
            live = {n: False for n in range(len(chains))}
        else:
            live = {n: h * ATTN_Q_COLS < (diag + 1) * tk
                    for n, (c, h) in enumerate(chains) if (h + 1) * ATTN_Q_COLS > diag * tk}
        scores = {}
        for n in live:
            c, h = chains[n]
            cols = slice(h * ATTN_Q_COLS, (h + 1) * ATTN_Q_COLS)
            scores[n] = jnp.dot(k[:, c * d:(c + 1) * d], q_t[c * d:(c + 1) * d, cols],
                                preferred_element_type=F32)
        m, l = list(m), list(l)
        for n, masked in live.items():
            c, h = chains[n]
            cols = slice(h * ATTN_Q_COLS, (h + 1) * ATTN_Q_COLS)
            s = scores[n]
            if masked:
                kv = lax.broadcasted_iota(jnp.int32, s.shape, 0) + diag * tk
                qq = lax.broadcasted_iota(jnp.int32, s.shape, 1) + h * ATTN_Q_COLS
                s = jnp.where(kv <= qq, s, jnp.finfo(F32).min)
            m_new = jnp.maximum(m[n], jnp.max(s, axis=0, keepdims=True))
            alpha = jnp.exp2(m[n] - m_new)
            p = jnp.exp2(s - m_new)
            l[n] = alpha * l[n] + jnp.sum(p, axis=0, keepdims=True)
            acc_ref[c, :, cols] = alpha * acc_ref[c, :, cols] + jnp.dot(
                v_t, p.astype(BF16), preferred_element_type=F32)
            m[n] = m_new
        return tuple(m), tuple(l)

    neg = jnp.full((1, ATTN_Q_COLS), -jnp.inf, F32)
    zero = jnp.zeros((1, ATTN_Q_COLS), F32)
    m, l = lax.fori_loop(0, first_diag, lambda j, ml: chunk(j, ml[0], ml[1]),
                         ((neg,) * len(chains), (zero,) * len(chains)))
    for t in range(diag_chunks):
        m, l = chunk(first_diag + t, m, l, diag=t)
    l = [jnp.concatenate([l[n] for n, (c, h) in enumerate(chains) if c == cc], axis=1)
         for cc in range(2)]

    lam = (jnp.exp(jnp.sum(lq1_ref[...] * lk1_ref[...], axis=-1, keepdims=True))
           - jnp.exp(jnp.sum(lq2_ref[...] * lk2_ref[...], axis=-1, keepdims=True)) + LAM_INIT)
    o_t = acc_ref[0] * (1.0 / l[0]) - acc_ref[1] * (lam / l[1])
    o_t = o_t * lax.rsqrt(jnp.mean(o_t * o_t, axis=0, keepdims=True) + EPS)
    o_ref[...] = (o_t.T * (g_ref[...] * (1.0 - LAM_INIT))).astype(o_ref.dtype)


def _diff_attention(proj, lq1, lk1, lq2, lk2, g_subln, batch, seq, tq=1024, tk=512):
    t = proj.shape[0]
    hw = 2 * ATTN_HEAD_DIM
    nq = seq // tq
    assert tq % tk == 0 and tk % ATTN_Q_COLS == 0
    k_off = ATTN_HEADS
    v_off = 2 * ATTN_HEADS
    vec = lambda a: a.reshape(1, -1)
    const = lambda b, h, i: (0, 0)
    return pl.pallas_call(
        _attn_kernel,
        out_shape=jax.ShapeDtypeStruct((t, ATTN_HEADS * hw), BF16),
        grid=(batch, ATTN_HEADS, nq),
        in_specs=[pl.BlockSpec((tq, hw), lambda b, h, i: (b * nq + i, h)),
                  pl.BlockSpec((seq, hw), lambda b, h, i: (b, k_off + h)),
                  pl.BlockSpec((seq, hw), lambda b, h, i: (b, v_off + h)),
                  pl.BlockSpec((1, ATTN_HEAD_DIM), const),
                  pl.BlockSpec((1, ATTN_HEAD_DIM), const),
                  pl.BlockSpec((1, ATTN_HEAD_DIM), const),
                  pl.BlockSpec((1, ATTN_HEAD_DIM), const),
                  pl.BlockSpec((1, hw), const)],
        out_specs=pl.BlockSpec((tq, hw), lambda b, h, i: (b * nq + i, h)),
        scratch_shapes=[pltpu.VMEM((seq // tk, hw, tk), BF16),
                        pltpu.VMEM((2, hw, tq), F32)],
        compiler_params=_params("parallel", "parallel", "arbitrary"),
        name="diff_attention",
    )(proj, proj, proj, vec(lq1), vec(lk1), vec(lq2), vec(lk2), vec(g_subln))


def _conv3(p, tail, w):
    w0, w1, w2 = w[0:1], w[1:2], w[2:3]
    body = pltpu.roll(p, 2, 0) * w0 + pltpu.roll(p, 1, 0) * w1 + p * w2
    first = p[0:SUBLANES]
    row = lax.broadcasted_iota(jnp.int32, first.shape, 0)
    h1 = jnp.where(row < 1, pltpu.roll(tail, 1, 0), pltpu.roll(first, 1, 0))
    h2 = jnp.where(row < 2, pltpu.roll(tail, 2, 0), pltpu.roll(first, 2, 0))
    head = jnp.concatenate([h2 * w0 + h1 * w1 + first * w2, body[SUBLANES:BF16_ROWS]], axis=0)
    return body, head


def _convmix_kernel(gb_ref, gc_ref, hc_ref, gct_ref, hct_ref, w_ref, o_ref, *, per_seq):
    i = pl.program_id(0)
    p = gc_ref[...].astype(F32) * hc_ref[...].astype(F32)
    tail = (gct_ref[...].astype(F32) * hct_ref[...].astype(F32))[BF16_ROWS - SUBLANES:]
    tail = jnp.where(i % per_seq == 0, 0.0, tail)
    body, head = _conv3(p, tail, w_ref[...])
    gb = gb_ref[...].astype(F32)
    o_ref[...] = (gb * body).astype(o_ref.dtype)
    o_ref[0:BF16_ROWS, :] = (gb[0:BF16_ROWS] * head).astype(o_ref.dtype)


def _conv_mixer(proj, w_conv, attn_width, conv_width, seq, tm=512, tc=512):
    t = proj.shape[0]
    per_seq = seq // tm
    cb = lambda off: (off // tc)
    b_off, c_off, h_off = cb(3 * attn_width), cb(3 * attn_width + conv_width), cb(3 * attn_width + 2 * conv_width)
    tail_blocks = tm // BF16_ROWS
    tail_idx = lambda i: jnp.maximum(i * tail_blocks - 1, 0)
    return pl.pallas_call(
        functools.partial(_convmix_kernel, per_seq=per_seq),
        out_shape=jax.ShapeDtypeStruct((t, conv_width), BF16),
        grid=(t // tm, conv_width // tc),
        in_specs=[pl.BlockSpec((tm, tc), lambda i, j: (i, b_off + j)),
                  pl.BlockSpec((tm, tc), lambda i, j: (i, c_off + j)),
                  pl.BlockSpec((tm, tc), lambda i, j: (i, h_off + j)),
                  pl.BlockSpec((BF16_ROWS, tc), lambda i, j: (tail_idx(i), c_off + j)),
                  pl.BlockSpec((BF16_ROWS, tc), lambda i, j: (tail_idx(i), h_off + j)),
                  pl.BlockSpec((CONV_K, tc), lambda i, j: (0, j))],
        out_specs=pl.BlockSpec((tm, tc), lambda i, j: (i, j)),
        compiler_params=_params("parallel", "parallel"),
        name="conv_mixer",
    )(proj, proj, proj, proj, proj, w_conv)


def _accumulate_dot(acc_ref, a, w_ref, first, k_rows=None):
    n = acc_ref.shape[1]
    for c in range(n // ACC_COLS):
        cols = slice(c * ACC_COLS, (c + 1) * ACC_COLS)
        w = w_ref[:, cols] if k_rows is None else w_ref[0:k_rows, cols]
        part = jnp.dot(a, w, preferred_element_type=F32)
        if first:
            acc_ref[:, cols] = part
        else:
            acc_ref[:, cols] += part


def _stream_epilogue(n_chunks, loads, stores, row_value, finish):
    for load in loads:
        load(0, 0).start()
    for r in range(n_chunks):
        slot = r % 2
        if r + 1 < n_chunks:
            for load in loads:
                load(r + 1, 1 - slot).start()
        rows = row_value(r)
        for load in loads:
            load(r, slot).wait()
        if r >= 2:
            for store in stores:
                store(r - 2, slot).wait()
        finish(slot, rows)
        for store in stores:
            store(r, slot).start()
    for r in range(max(n_chunks - 2, 0), n_chunks):
        for store in stores:
            store(r, r % 2).wait()


def _row_copy(hbm_ref, buf_ref, sem_ref, row0, to_hbm):
    rows_per = buf_ref.shape[1]

    def make(chunk, slot):
        hbm = hbm_ref.at[pl.ds(row0 + chunk * rows_per, rows_per), :]
        if to_hbm:
            return pltpu.make_async_copy(buf_ref.at[slot], hbm, sem_ref.at[slot])
        return pltpu.make_async_copy(hbm, buf_ref.at[slot], sem_ref.at[slot])
    return make


def _outproj_kernel(a_ref, c_ref, w_ref, gt_ref, gpost_ref, gpre_ref, sc_ref, sh_ref, x_hbm,
                    x1_hbm, h2_hbm, acc_ref, x_buf, x1_buf, h2_buf, x_sem, x1_sem, h2_sem,
                    *, nk_attn):
    i = pl.program_id(0)
    k = pl.program_id(1)
    nk = pl.num_programs(1)
    tm = acc_ref.shape[0]

    @pl.when(k == 0)
    def _():
        _accumulate_dot(acc_ref, a_ref[...], w_ref, first=True)

    @pl.when((k > 0) & (k < nk_attn))
    def _():
        _accumulate_dot(acc_ref, a_ref[...], w_ref, first=False)

    @pl.when((k >= nk_attn) & (k < nk - 1))
    def _():
        _accumulate_dot(acc_ref, c_ref[...], w_ref, first=False)

    @pl.when(k == nk - 1)
    def _():
        row0 = pl.multiple_of(i * tm, tm)

        def row_value(r):
            rows = slice(r * EPILOGUE_ROWS, (r + 1) * EPILOGUE_ROWS)
            return acc_ref[rows, :] + jnp.dot(c_ref[rows, :], w_ref[...],
                                              preferred_element_type=F32)

        post_gain = gt_ref[0] * gpost_ref[...]
        pre_gain = gpre_ref[...] * (1.0 + sc_ref[0])
        shift = sh_ref[0]

        def finish(slot, y):
            x1 = x_buf[slot] + (y * _rms_scale(y)) * post_gain
            x1_buf[slot] = x1
            h2_buf[slot] = ((x1 * _rms_scale(x1)) * pre_gain + shift).astype(h2_buf.dtype)

        _stream_epilogue(tm // EPILOGUE_ROWS,
                         [_row_copy(x_hbm, x_buf, x_sem, row0, to_hbm=False)],
                         [_row_copy(x1_hbm, x1_buf, x1_sem, row0, to_hbm=True),
                          _row_copy(h2_hbm, h2_buf, h2_sem, row0, to_hbm=True)],
                         row_value, finish)


def _outproj(attn, conv, w_out, x2, gt, g_post, g_pre, sc, sh, seq, tm=1024, tk=512):
    t, d = x2.shape
    ka = attn.shape[1]
    nk_attn = ka // tk
    nk = d // tk
    assert nk - 1 > nk_attn >= 1
    per_seq = seq // tm
    row = lambda i, k: (0, 0)
    mod = lambda i, k: (i // per_seq, 0, 0)
    hbm = pl.BlockSpec(memory_space=pl.ANY)
    return pl.pallas_call(
        functools.partial(_outproj_kernel, nk_attn=nk_attn),
        out_shape=(jax.ShapeDtypeStruct((t, d), F32), jax.ShapeDtypeStruct((t, d), BF16)),
        grid=(t // tm, nk),
        in_specs=[pl.BlockSpec((tm, tk), lambda i, k: (i, jnp.minimum(k, nk_attn - 1))),
                  pl.BlockSpec((tm, tk), lambda i, k: (i, jnp.maximum(k - nk_attn, 0))),
                  pl.BlockSpec((tk, d), lambda i, k: (k, 0)),
                  pl.BlockSpec((1, 1, d), mod),
                  pl.BlockSpec((1, d), row),
                  pl.BlockSpec((1, d), row),
                  pl.BlockSpec((1, 1, d), mod),
                  pl.BlockSpec((1, 1, d), mod),
                  hbm],
        out_specs=(hbm, hbm),
        scratch_shapes=[pltpu.VMEM((tm, d), F32),
                        pltpu.VMEM((2, EPILOGUE_ROWS, d), F32),
                        pltpu.VMEM((2, EPILOGUE_ROWS, d), F32),
                        pltpu.VMEM((2, EPILOGUE_ROWS, d), BF16),
                        pltpu.SemaphoreType.DMA((2,)),
                        pltpu.SemaphoreType.DMA((2,)),
                        pltpu.SemaphoreType.DMA((2,))],
        compiler_params=_params("parallel", "arbitrary"),
        name="outproj_norms",
    )(attn, conv, w_out, gt, g_post.reshape(1, d), g_pre.reshape(1, d), sc, sh, x2)


def _upproj_kernel(h_ref, wg32_ref, wv32_ref, cg_ref, cv_ref, o_ref, tail_g, tail_v, wg_ref, wv_ref,
                   *, per_seq):
    i = pl.program_id(0)
    j = pl.program_id(1)
    wg_ref[...] = wg32_ref[...].astype(BF16)
    wv_ref[...] = wv32_ref[...].astype(BF16)

    @pl.when(i % per_seq == 0)
    def _():
        tail_g[j] = jnp.zeros(tail_g.shape[1:], F32)
        tail_v[j] = jnp.zeros(tail_v.shape[1:], F32)

    tg = tail_g[j]
    tv = tail_v[j]
    cg = cg_ref[...]
    cv = cv_ref[...]
    for r in range(o_ref.shape[0] // DOT_ROWS):
        rows = slice(r * DOT_ROWS, (r + 1) * DOT_ROWS)
        h = h_ref[rows, :]
        ug = jnp.dot(h, wg_ref[...], preferred_element_type=F32)
        uv = jnp.dot(h, wv_ref[...], preferred_element_type=F32)
        g_body, g_head = _conv3(ug, tg, cg)
        v_body, v_head = _conv3(uv, tv, cv)
        tg = ug[DOT_ROWS - SUBLANES:]
        tv = uv[DOT_ROWS - SUBLANES:]
        o_ref[rows, :] = (g_body * jax.nn.sigmoid(g_body) * v_body).astype(o_ref.dtype)
        head = slice(r * DOT_ROWS, r * DOT_ROWS + BF16_ROWS)
        o_ref[head, :] = (g_head * jax.nn.sigmoid(g_head) * v_head).astype(o_ref.dtype)
    tail_g[j] = tg
    tail_v[j] = tv


def _upproj(h2, w_up, w_conv, d_ff, seq, tm=2048, tn=256):
    t, d = h2.shape
    nj = d_ff // tn
    per_seq = seq // tm
    return pl.pallas_call(
        functools.partial(_upproj_kernel, per_seq=per_seq),
        out_shape=jax.ShapeDtypeStruct((t, d_ff), BF16),
        grid=(t // tm, nj),
        in_specs=[pl.BlockSpec((tm, d), lambda i, j: (i, 0), pipeline_mode=pl.Buffered(1)),
                  pl.BlockSpec((d, tn), lambda i, j: (0, j)),
                  pl.BlockSpec((d, tn), lambda i, j: (0, nj + j)),
                  pl.BlockSpec((CONV_K, tn), lambda i, j: (0, j)),
                  pl.BlockSpec((CONV_K, tn), lambda i, j: (0, nj + j))],
        out_specs=pl.BlockSpec((tm, tn), lambda i, j: (i, j)),
        scratch_shapes=[pltpu.VMEM((nj, SUBLANES, tn), F32),
                        pltpu.VMEM((nj, SUBLANES, tn), F32),
                        pltpu.VMEM((d, tn), BF16),
                        pltpu.VMEM((d, tn), BF16)],
        compiler_params=_params("arbitrary", "arbitrary"),
        name="upproj_conv_gate",
    )(h2, w_up, w_up, w_conv, w_conv)


def _downproj_kernel(g_ref, w_ref, gt_ref, gpost_ref, x1_hbm, o_hbm, acc_ref, x1_buf, o_buf,
                     x1_sem, o_sem, *, nk, k_last):
    i = pl.program_id(0)
    k = pl.program_id(1)
    tm = acc_ref.shape[0]

    @pl.when(k == 0)
    def _():
        _accumulate_dot(acc_ref, g_ref[...], w_ref, first=True)

    @pl.when((k > 0) & (k < nk - 1))
    def _():
        _accumulate_dot(acc_ref, g_ref[...], w_ref, first=False)

    @pl.when(k == nk - 1)
    def _():
        row0 = pl.multiple_of(i * tm, tm)

        def row_value(r):
            rows = slice(r * EPILOGUE_ROWS, (r + 1) * EPILOGUE_ROWS)
            return acc_ref[rows, :] + jnp.dot(g_ref[rows, 0:k_last], w_ref[0:k_last, :],
                                              preferred_element_type=F32)

        post_gain = gt_ref[0] * gpost_ref[...]

        def finish(slot, f):
            o_buf[slot] = x1_buf[slot] + (f * _rms_scale(f)) * post_gain

        _stream_epilogue(tm // EPILOGUE_ROWS,
                         [_row_copy(x1_hbm, x1_buf, x1_sem, row0, to_hbm=False)],
                         [_row_copy(o_hbm, o_buf, o_sem, row0, to_hbm=True)],
                         row_value, finish)


def _downproj(g, w_down, x1, gt, g_post, seq, tm=1024, tk=1024):
    t, d = x1.shape
    d_ff = g.shape[1]
    nk = pl.cdiv(d_ff, tk)
    k_last = d_ff - (nk - 1) * tk
    assert nk >= 3
    per_seq = seq // tm
    hbm = pl.BlockSpec(memory_space=pl.ANY)
    return pl.pallas_call(
        functools.partial(_downproj_kernel, nk=nk, k_last=k_last),
        out_shape=jax.ShapeDtypeStruct((t, d), F32),
        grid=(t // tm, nk),
        in_specs=[pl.BlockSpec((tm, tk), lambda i, k: (i, k)),
                  pl.BlockSpec((tk, d), lambda i, k: (k, 0)),
                  pl.BlockSpec((1, 1, d), lambda i, k: (i // per_seq, 0, 0)),
                  pl.BlockSpec((1, d), lambda i, k: (0, 0)),
                  hbm],
        out_specs=hbm,
        scratch_shapes=[pltpu.VMEM((tm, d), F32),
                        pltpu.VMEM((2, EPILOGUE_ROWS, d), F32),
                        pltpu.VMEM((2, EPILOGUE_ROWS, d), F32),
                        pltpu.SemaphoreType.DMA((2,)),
                        pltpu.SemaphoreType.DMA((2,))],
        compiler_params=_params("parallel", "arbitrary"),
        name="downproj_norm",
    )(g, w_down, gt, g_post.reshape(1, d), x1)


def kernel(x, c, positions, w_ada, b_ada, g_pre_mix, g_post_mix, w_in, lambda_q1, lambda_k1, lambda_q2, lambda_k2, g_subln, w_conv_mix, w_out, g_pre_ffn, g_post_ffn, w_up, w_conv_ffn, w_down):
    batch, seq, d = x.shape
    depth = w_ada.shape[0]
    assert depth == 1, "the attention lambda schedule below is written for a single layer"
    attn_width = ATTN_HEADS * 2 * ATTN_HEAD_DIM
    conv_width = d - attn_width
    d_ff = w_down.shape[1]
    t = batch * seq

    cos, sin = _rope_tables(positions)
    x2 = x.reshape(t, d)
    for l in range(depth):
        mod = _ada_modulation(c, w_ada[l], b_ada[l])
        sh1, sc1, gt1, sh2, sc2, gt2 = [m.reshape(batch, 1, d) for m in jnp.split(mod, N_MOD, axis=-1)]

        h = _prenorm(x2, g_pre_mix[l], sc1, sh1, seq)
        proj = _inproj(h, w_in[l], cos, sin, attn_width)
        attn = _diff_attention(proj, lambda_q1[l], lambda_k1[l], lambda_q2[l], lambda_k2[l],
                               g_subln[l], batch, seq)
        conv = _conv_mixer(proj, w_conv_mix[l], attn_width, conv_width, seq)
        x1, h2 = _outproj(attn, conv, w_out[l].astype(BF16), x2, gt1, g_post_mix[l], g_pre_ffn[l],
                          sc2, sh2, seq)
        gated = _upproj(h2, w_up[l], w_conv_ffn[l], d_ff, seq)
        x2 = _downproj(gated, w_down[l].astype(BF16), x1, gt2, g_post_ffn[l], seq)
    return x2.reshape(batch, seq, d)
```

```python
import functools
import math

import jax
import jax.numpy as jnp
from jax import lax
from jax.experimental import pallas as pl
from jax.experimental.pallas import tpu as pltpu

F32 = jnp.float32
BF16 = jnp.bfloat16

ATTN_HEADS = 8
ATTN_HEAD_DIM = 128
CONV_K = 3
ROPE_THETA = 10000.0
EPS = 1e-6
N_MOD = 6
LAM_INIT = 0.8 - 0.6 * math.exp(-0.3 * 0)

LANES = 128
SUBLANES = 8
BF16_ROWS = 16
VMEM_LIMIT = 56 * 1024 * 1024
ACC_COLS = 1024
EPILOGUE_ROWS = 128
DOT_ROWS = 128
ATTN_Q_COLS = 256


def _params(*sem):
    return pltpu.CompilerParams(dimension_semantics=sem, vmem_limit_bytes=VMEM_LIMIT)


def _rms_scale(v):
    return lax.rsqrt(jnp.mean(v * v, axis=-1, keepdims=True) + EPS)


def _ada_kernel(c_ref, w_ref, b_ref, o_ref):
    c = c_ref[...]
    cond = c * jax.nn.sigmoid(c)
    o_ref[...] = jnp.dot(cond.astype(BF16), w_ref[...].astype(BF16),
                         preferred_element_type=F32) + b_ref[...]


def _ada_modulation(c, w_ada, b_ada, tn=512):
    b, d = c.shape
    n = w_ada.shape[1]
    rows = -(-b // SUBLANES) * SUBLANES
    c_pad = jnp.pad(c, ((0, rows - b), (0, 0)))
    out = pl.pallas_call(
        _ada_kernel,
        out_shape=jax.ShapeDtypeStruct((rows, n), F32),
        grid=(n // tn,),
        in_specs=[pl.BlockSpec((rows, d), lambda j: (0, 0)),
                  pl.BlockSpec((d, tn), lambda j: (0, j)),
                  pl.BlockSpec((1, tn), lambda j: (0, j))],
        out_specs=pl.BlockSpec((rows, tn), lambda j: (0, j)),
        compiler_params=_params("parallel"),
        name="ada_modulation",
    )(c_pad, w_ada, b_ada.reshape(1, n))
    return out[:b]


def _rope_kernel(pos_ref, inv_ref, cos_ref, sin_ref):
    ang = pos_ref[...].astype(F32) * inv_ref[...]
    lane = lax.broadcasted_iota(jnp.int32, ang.shape, 1)
    sin = jnp.sin(ang)
    cos_ref[...] = jnp.cos(ang)
    sin_ref[...] = jnp.where(lane < ATTN_HEAD_DIM // 2, -sin, sin)


def _rope_tables(positions, tm=2048):
    t = positions.size
    dim = ATTN_HEAD_DIM
    inv_freq = ROPE_THETA ** (-jnp.arange(0, dim, 2, dtype=F32) / dim)
    inv = jnp.concatenate([inv_freq, inv_freq]).reshape(1, dim)
    return pl.pallas_call(
        _rope_kernel,
        out_shape=(jax.ShapeDtypeStruct((t, dim), F32), jax.ShapeDtypeStruct((t, dim), F32)),
        grid=(t // tm,),
        in_specs=[pl.BlockSpec((tm, 1), lambda i: (i, 0)),
                  pl.BlockSpec((1, dim), lambda i: (0, 0))],
        out_specs=(pl.BlockSpec((tm, dim), lambda i: (i, 0)),
                   pl.BlockSpec((tm, dim), lambda i: (i, 0))),
        compiler_params=_params("parallel"),
        name="rope_tables",
    )(positions.reshape(t, 1), inv)


def _prenorm_kernel(x_ref, g_ref, sc_ref, sh_ref, o_ref):
    x = x_ref[...]
    y = x * _rms_scale(x) * g_ref[...]
    o_ref[...] = (y * (1.0 + sc_ref[0]) + sh_ref[0]).astype(o_ref.dtype)


def _prenorm(x2, g, sc, sh, seq, tm=512):
    t, d = x2.shape
    per_seq = seq // tm
    return pl.pallas_call(
        _prenorm_kernel,
        out_shape=jax.ShapeDtypeStruct((t, d), BF16),
        grid=(t // tm,),
        in_specs=[pl.BlockSpec((tm, d), lambda i: (i, 0)),
                  pl.BlockSpec((1, d), lambda i: (0, 0)),
                  pl.BlockSpec((1, 1, d), lambda i: (i // per_seq, 0, 0)),
                  pl.BlockSpec((1, 1, d), lambda i: (i // per_seq, 0, 0))],
        out_specs=pl.BlockSpec((tm, d), lambda i: (i, 0)),
        compiler_params=_params("parallel"),
        name="prenorm",
    )(x2, g.reshape(1, d), sc, sh)


def _inproj_kernel(h_ref, w32_ref, cos_ref, sin_ref, o_ref, w_ref, *, n_q, n_qk, q_scale):
    j = pl.program_id(1)
    tm, tn = o_ref.shape
    w_ref[...] = w32_ref[...].astype(BF16)

    @pl.when(j >= n_qk)
    def _():
        for r in range(tm // DOT_ROWS):
            rows = slice(r * DOT_ROWS, (r + 1) * DOT_ROWS)
            acc = jnp.dot(h_ref[rows, :], w_ref[...], preferred_element_type=F32)
            o_ref[rows, :] = acc.astype(o_ref.dtype)

    @pl.when(j < n_qk)
    def _():
        scale = jnp.where(j < n_q, q_scale, 1.0).astype(F32)
        for r in range(tm // DOT_ROWS):
            rows = slice(r * DOT_ROWS, (r + 1) * DOT_ROWS)
            acc = jnp.dot(h_ref[rows, :], w_ref[...], preferred_element_type=F32)
            cos = cos_ref[rows, :]
            sin = sin_ref[rows, :] * scale
            cos = cos * scale
            for c in range(tn // LANES):
                t = acc[:, c * LANES:(c + 1) * LANES]
                rot = t * cos + pltpu.roll(t, ATTN_HEAD_DIM // 2, 1) * sin
                o_ref[rows, c * LANES:(c + 1) * LANES] = rot.astype(o_ref.dtype)


def _inproj(h, w_in, cos, sin, attn_width, tm=2048, tn=512):
    t, d = h.shape
    n = w_in.shape[1]
    kern = functools.partial(_inproj_kernel, n_q=attn_width // tn, n_qk=2 * attn_width // tn,
                             q_scale=ATTN_HEAD_DIM ** -0.5 * math.log2(math.e))
    return pl.pallas_call(
        kern,
        out_shape=jax.ShapeDtypeStruct((t, n), BF16),
        grid=(t // tm, n // tn),
        in_specs=[pl.BlockSpec((tm, d), lambda i, j: (i, 0), pipeline_mode=pl.Buffered(1)),
                  pl.BlockSpec((d, tn), lambda i, j: (0, j)),
                  pl.BlockSpec((tm, ATTN_HEAD_DIM), lambda i, j: (i, 0)),
                  pl.BlockSpec((tm, ATTN_HEAD_DIM), lambda i, j: (i, 0))],
        out_specs=pl.BlockSpec((tm, tn), lambda i, j: (i, j)),
        scratch_shapes=[pltpu.VMEM((d, tn), BF16)],
        compiler_params=_params("parallel", "parallel"),
        name="inproj_rope",
    )(h, w_in, cos, sin)


def _attn_kernel(q_ref, k_ref, v_ref, lq1_ref, lk1_ref, lq2_ref, lk2_ref, g_ref,
                 o_ref, vt_ref, acc_ref, sa_ref, sb_ref):
    qi = pl.program_id(2)
    tq = q_ref.shape[0]
    tk = vt_ref.shape[2]
    d = ATTN_HEAD_DIM
    diag_chunks = tq // tk
    first_diag = qi * diag_chunks

    for t in range(diag_chunks):
        rows = pl.ds(pl.multiple_of((first_diag + t) * tk, tk), tk)
        vt_ref[first_diag + t] = v_ref[rows, :].T
    q_t = q_ref[...].T
    acc_ref[...] = jnp.zeros(acc_ref.shape, F32)

    chains = [(c, h) for c in range(2) for h in range(tq // ATTN_Q_COLS)]

    def live_chains(diag):
        if diag is None:
            return {n: False for n in range(len(chains))}
        return {n: h * ATTN_Q_COLS < (diag + 1) * tk
                for n, (c, h) in enumerate(chains) if (h + 1) * ATTN_Q_COLS > diag * tk}

    def score_dots(j, s_ref, diag=None):
        k = k_ref[pl.ds(pl.multiple_of(j * tk, tk), tk), :]
        for n in live_chains(diag):
            c, h = chains[n]
            cols = slice(h * ATTN_Q_COLS, (h + 1) * ATTN_Q_COLS)
            s_ref[n] = jnp.dot(k[:, c * d:(c + 1) * d], q_t[c * d:(c + 1) * d, cols],
                               preferred_element_type=F32)

    def fold(j, s_ref, m, l, diag=None):
        v_t = vt_ref[j]
        m, l = list(m), list(l)
        for n, masked in live_chains(diag).items():
            c, h = chains[n]
            cols = slice(h * ATTN_Q_COLS, (h + 1) * ATTN_Q_COLS)
            s = s_ref[n]
            if masked:
                kv = lax.broadcasted_iota(jnp.int32, s.shape, 0) + diag * tk
                qq = lax.broadcasted_iota(jnp.int32, s.shape, 1) + h * ATTN_Q_COLS
                s = jnp.where(kv <= qq, s, jnp.finfo(F32).min)
            m_new = jnp.maximum(m[n], jnp.max(s, axis=0, keepdims=True))
            alpha = jnp.exp2(m[n] - m_new)
            p = jnp.exp2(s - m_new)
            l[n] = alpha * l[n] + jnp.sum(p, axis=0, keepdims=True)
            acc_ref[c, :, cols] = alpha * acc_ref[c, :, cols] + jnp.dot(
                v_t, p.astype(BF16), preferred_element_type=F32)
            m[n] = m_new
        return tuple(m), tuple(l)

    neg = jnp.full((1, ATTN_Q_COLS), -jnp.inf, F32)
    zero = jnp.zeros((1, ATTN_Q_COLS), F32)

    def pair(i, ml):
        m, l = ml
        score_dots(2 * i + 1, sb_ref)
        m, l = fold(2 * i, sa_ref, m, l)
        score_dots(2 * i + 2, sa_ref)
        return fold(2 * i + 1, sb_ref, m, l)

    score_dots(0, sa_ref)
    m, l = lax.fori_loop(0, first_diag // 2, pair,
                         ((neg,) * len(chains), (zero,) * len(chains)))
    for t in range(diag_chunks):
        cur, nxt = (sa_ref, sb_ref) if t % 2 == 0 else (sb_ref, sa_ref)
        if t + 1 < diag_chunks:
            score_dots(first_diag + t + 1, nxt, diag=t + 1)
        m, l = fold(first_diag + t, cur, m, l, diag=t)
    l = [jnp.concatenate([l[n] for n, (c, h) in enumerate(chains) if c == cc], axis=1)
         for cc in range(2)]

    lam = (jnp.exp(jnp.sum(lq1_ref[...] * lk1_ref[...], axis=-1, keepdims=True))
           - jnp.exp(jnp.sum(lq2_ref[...] * lk2_ref[...], axis=-1, keepdims=True)) + LAM_INIT)
    o_t = acc_ref[0] * (1.0 / l[0]) - acc_ref[1] * (lam / l[1])
    o_t = o_t * lax.rsqrt(jnp.mean(o_t * o_t, axis=0, keepdims=True) + EPS)
    o_ref[...] = (o_t.T * (g_ref[...] * (1.0 - LAM_INIT))).astype(o_ref.dtype)


def _diff_attention(proj, lq1, lk1, lq2, lk2, g_subln, batch, seq, tq=1024, tk=512):
    t = proj.shape[0]
    hw = 2 * ATTN_HEAD_DIM
    nq = seq // tq
    assert tq % (2 * tk) == 0 and tk % ATTN_Q_COLS == 0
    n_chains = 2 * tq // ATTN_Q_COLS
    k_off = ATTN_HEADS
    v_off = 2 * ATTN_HEADS
    vec = lambda a: a.reshape(1, -1)
    const = lambda b, h, i: (0, 0)
    return pl.pallas_call(
        _attn_kernel,
        out_shape=jax.ShapeDtypeStruct((t, ATTN_HEADS * hw), BF16),
        grid=(batch, ATTN_HEADS, nq),
        in_specs=[pl.BlockSpec((tq, hw), lambda b, h, i: (b * nq + i, h)),
                  pl.BlockSpec((seq, hw), lambda b, h, i: (b, k_off + h)),
                  pl.BlockSpec((seq, hw), lambda b, h, i: (b, v_off + h)),
                  pl.BlockSpec((1, ATTN_HEAD_DIM), const),
                  pl.BlockSpec((1, ATTN_HEAD_DIM), const),
                  pl.BlockSpec((1, ATTN_HEAD_DIM), const),
                  pl.BlockSpec((1, ATTN_HEAD_DIM), const),
                  pl.BlockSpec((1, hw), const)],
        out_specs=pl.BlockSpec((tq, hw), lambda b, h, i: (b * nq + i, h)),
        scratch_shapes=[pltpu.VMEM((seq // tk, hw, tk), BF16),
                        pltpu.VMEM((2, hw, tq), F32),
                        pltpu.VMEM((n_chains, tk, ATTN_Q_COLS), F32),
                        pltpu.VMEM((n_chains, tk, ATTN_Q_COLS), F32)],
        compiler_params=_params("parallel", "parallel", "arbitrary"),
        name="diff_attention",
    )(proj, proj, proj, vec(lq1), vec(lk1), vec(lq2), vec(lk2), vec(g_subln))


def _conv3(p, tail, w):
    w0, w1, w2 = w[0:1], w[1:2], w[2:3]
    body = pltpu.roll(p, 2, 0) * w0 + pltpu.roll(p, 1, 0) * w1 + p * w2
    first = p[0:SUBLANES]
    row = lax.broadcasted_iota(jnp.int32, first.shape, 0)
    h1 = jnp.where(row < 1, pltpu.roll(tail, 1, 0), pltpu.roll(first, 1, 0))
    h2 = jnp.where(row < 2, pltpu.roll(tail, 2, 0), pltpu.roll(first, 2, 0))
    head = jnp.concatenate([h2 * w0 + h1 * w1 + first * w2, body[SUBLANES:BF16_ROWS]], axis=0)
    return body, head


def _convmix_kernel(gb_ref, gc_ref, hc_ref, gct_ref, hct_ref, w_ref, o_ref, *, per_seq):
    i = pl.program_id(0)
    p = gc_ref[...].astype(F32) * hc_ref[...].astype(F32)
    tail = (gct_ref[...].astype(F32) * hct_ref[...].astype(F32))[BF16_ROWS - SUBLANES:]
    tail = jnp.where(i % per_seq == 0, 0.0, tail)
    body, head = _conv3(p, tail, w_ref[...])
    gb = gb_ref[...].astype(F32)
    o_ref[...] = (gb * body).astype(o_ref.dtype)
    o_ref[0:BF16_ROWS, :] = (gb[0:BF16_ROWS] * head).astype(o_ref.dtype)


def _conv_mixer(proj, w_conv, attn_width, conv_width, seq, tm=1024, tc=1024):
    t = proj.shape[0]
    per_seq = seq // tm
    cb = lambda off: (off // tc)
    b_off, c_off, h_off = cb(3 * attn_width), cb(3 * attn_width + conv_width), cb(3 * attn_width + 2 * conv_width)
    tail_blocks = tm // BF16_ROWS
    tail_idx = lambda i: jnp.maximum(i * tail_blocks - 1, 0)
    return pl.pallas_call(
        functools.partial(_convmix_kernel, per_seq=per_seq),
        out_shape=jax.ShapeDtypeStruct((t, conv_width), BF16),
        grid=(t // tm, conv_width // tc),
        in_specs=[pl.BlockSpec((tm, tc), lambda i, j: (i, b_off + j)),
                  pl.BlockSpec((tm, tc), lambda i, j: (i, c_off + j)),
                  pl.BlockSpec((tm, tc), lambda i, j: (i, h_off + j)),
                  pl.BlockSpec((BF16_ROWS, tc), lambda i, j: (tail_idx(i), c_off + j)),
                  pl.BlockSpec((BF16_ROWS, tc), lambda i, j: (tail_idx(i), h_off + j)),
                  pl.BlockSpec((CONV_K, tc), lambda i, j: (0, j))],
        out_specs=pl.BlockSpec((tm, tc), lambda i, j: (i, j)),
        compiler_params=_params("parallel", "parallel"),
        name="conv_mixer",
    )(proj, proj, proj, proj, proj, w_conv)


def _accumulate_dot(acc_ref, a, w_ref, first, k_rows=None):
    n = acc_ref.shape[1]
    for c in range(n // ACC_COLS):
        cols = slice(c * ACC_COLS, (c + 1) * ACC_COLS)
        w = w_ref[:, cols] if k_rows is None else w_ref[0:k_rows, cols]
        part = jnp.dot(a, w, preferred_element_type=F32)
        if first:
            acc_ref[:, cols] = part
        else:
            acc_ref[:, cols] += part


def _stream_epilogue(n_chunks, loads, stores, row_value, finish):
    for load in loads:
        load(0, 0).start()
    for r in range(n_chunks):
        slot = r % 2
        if r + 1 < n_chunks:
            for load in loads:
                load(r + 1, 1 - slot).start()
        rows = row_value(r)
        for load in loads:
            load(r, slot).wait()
        if r >= 2:
            for store in stores:
                store(r - 2, slot).wait()
        finish(slot, rows)
        for store in stores:
            store(r, slot).start()
    for r in range(max(n_chunks - 2, 0), n_chunks):
        for store in stores:
            store(r, r % 2).wait()


def _row_copy(hbm_ref, buf_ref, sem_ref, row0, to_hbm):
    rows_per = buf_ref.shape[1]

    def make(chunk, slot):
        hbm = hbm_ref.at[pl.ds(row0 + chunk * rows_per, rows_per), :]
        if to_hbm:
            return pltpu.make_async_copy(buf_ref.at[slot], hbm, sem_ref.at[slot])
        return pltpu.make_async_copy(hbm, buf_ref.at[slot], sem_ref.at[slot])
    return make


def _outproj_kernel(a_ref, c_ref, w_ref, gt_ref, gpost_ref, gpre_ref, sc_ref, sh_ref, x_hbm,
                    x1_hbm, h2_hbm, acc_ref, x_buf, x1_buf, h2_buf, x_sem, x1_sem, h2_sem,
                    *, nk_attn):
    i = pl.program_id(0)
    k = pl.program_id(1)
    nk = pl.num_programs(1)
    tm = acc_ref.shape[0]

    @pl.when(k == 0)
    def _():
        _accumulate_dot(acc_ref, a_ref[...], w_ref, first=True)

    @pl.when((k > 0) & (k < nk_attn))
    def _():
        _accumulate_dot(acc_ref, a_ref[...], w_ref, first=False)

    @pl.when((k >= nk_attn) & (k < nk - 1))
    def _():
        _accumulate_dot(acc_ref, c_ref[...], w_ref, first=False)

    @pl.when(k == nk - 1)
    def _():
        row0 = pl.multiple_of(i * tm, tm)

        def row_value(r):
            rows = slice(r * EPILOGUE_ROWS, (r + 1) * EPILOGUE_ROWS)
            return acc_ref[rows, :] + jnp.dot(c_ref[rows, :], w_ref[...],
                                              preferred_element_type=F32)

        post_gain = gt_ref[0] * gpost_ref[...]
        pre_gain = gpre_ref[...] * (1.0 + sc_ref[0])
        shift = sh_ref[0]

        def finish(slot, y):
            x1 = x_buf[slot] + (y * _rms_scale(y)) * post_gain
            x1_buf[slot] = x1
            h2_buf[slot] = ((x1 * _rms_scale(x1)) * pre_gain + shift).astype(h2_buf.dtype)

        _stream_epilogue(tm // EPILOGUE_ROWS,
                         [_row_copy(x_hbm, x_buf, x_sem, row0, to_hbm=False)],
                         [_row_copy(x1_hbm, x1_buf, x1_sem, row0, to_hbm=True),
                          _row_copy(h2_hbm, h2_buf, h2_sem, row0, to_hbm=True)],
                         row_value, finish)


def _outproj(attn, conv, w_out, x2, gt, g_post, g_pre, sc, sh, seq, tm=1024, tk=512):
    t, d = x2.shape
    ka = attn.shape[1]
    nk_attn = ka // tk
    nk = d // tk
    assert nk - 1 > nk_attn >= 1
    per_seq = seq // tm
    row = lambda i, k: (0, 0)
    mod = lambda i, k: (i // per_seq, 0, 0)
    hbm = pl.BlockSpec(memory_space=pl.ANY)
    return pl.pallas_call(
        functools.partial(_outproj_kernel, nk_attn=nk_attn),
        out_shape=(jax.ShapeDtypeStruct((t, d), F32), jax.ShapeDtypeStruct((t, d), BF16)),
        grid=(t // tm, nk),
        in_specs=[pl.BlockSpec((tm, tk), lambda i, k: (i, jnp.minimum(k, nk_attn - 1))),
                  pl.BlockSpec((tm, tk), lambda i, k: (i, jnp.maximum(k - nk_attn, 0))),
                  pl.BlockSpec((tk, d), lambda i, k: (k, 0)),
                  pl.BlockSpec((1, 1, d), mod),
                  pl.BlockSpec((1, d), row),
                  pl.BlockSpec((1, d), row),
                  pl.BlockSpec((1, 1, d), mod),
                  pl.BlockSpec((1, 1, d), mod),
                  hbm],
        out_specs=(hbm, hbm),
        scratch_shapes=[pltpu.VMEM((tm, d), F32),
                        pltpu.VMEM((2, EPILOGUE_ROWS, d), F32),
                        pltpu.VMEM((2, EPILOGUE_ROWS, d), F32),
                        pltpu.VMEM((2, EPILOGUE_ROWS, d), BF16),
                        pltpu.SemaphoreType.DMA((2,)),
                        pltpu.SemaphoreType.DMA((2,)),
                        pltpu.SemaphoreType.DMA((2,))],
        compiler_params=_params("parallel", "arbitrary"),
        name="outproj_norms",
    )(attn, conv, w_out, gt, g_post.reshape(1, d), g_pre.reshape(1, d), sc, sh, x2)


def _upproj_kernel(h_ref, wg32_ref, wv32_ref, cg_ref, cv_ref, o_ref, tail_g, tail_v, wg_ref, wv_ref,
                   *, per_seq):
    i = pl.program_id(0)
    j = pl.program_id(1)
    wg_ref[...] = wg32_ref[...].astype(BF16)
    wv_ref[...] = wv32_ref[...].astype(BF16)

    @pl.when(i % per_seq == 0)
    def _():
        tail_g[j] = jnp.zeros(tail_g.shape[1:], F32)
        tail_v[j] = jnp.zeros(tail_v.shape[1:], F32)

    tg = tail_g[j]
    tv = tail_v[j]
    cg = cg_ref[...]
    cv = cv_ref[...]
    for r in range(o_ref.shape[0] // DOT_ROWS):
        rows = slice(r * DOT_ROWS, (r + 1) * DOT_ROWS)
        h = h_ref[rows, :]
        ug = jnp.dot(h, wg_ref[...], preferred_element_type=F32)
        uv = jnp.dot(h, wv_ref[...], preferred_element_type=F32)
        g_body, g_head = _conv3(ug, tg, cg)
        v_body, v_head = _conv3(uv, tv, cv)
        tg = ug[DOT_ROWS - SUBLANES:]
        tv = uv[DOT_ROWS - SUBLANES:]
        o_ref[rows, :] = (g_body * jax.nn.sigmoid(g_body) * v_body).astype(o_ref.dtype)
        head = slice(r * DOT_ROWS, r * DOT_ROWS + BF16_ROWS)
        o_ref[head, :] = (g_head * jax.nn.sigmoid(g_head) * v_head).astype(o_ref.dtype)
    tail_g[j] = tg
    tail_v[j] = tv


def _upproj(h2, w_up, w_conv, d_ff, seq, tm=2048, tn=256):
    t, d = h2.shape
    nj = d_ff // tn
    per_seq = seq // tm
    return pl.pallas_call(
        functools.partial(_upproj_kernel, per_seq=per_seq),
        out_shape=jax.ShapeDtypeStruct((t, d_ff), BF16),
        grid=(t // tm, nj),
        in_specs=[pl.BlockSpec((tm, d), lambda i, j: (i, 0), pipeline_mode=pl.Buffered(1)),
                  pl.BlockSpec((d, tn), lambda i, j: (0, j)),
                  pl.BlockSpec((d, tn), lambda i, j: (0, nj + j)),
                  pl.BlockSpec((CONV_K, tn), lambda i, j: (0, j)),
                  pl.BlockSpec((CONV_K, tn), lambda i, j: (0, nj + j))],
        out_specs=pl.BlockSpec((tm, tn), lambda i, j: (i, j)),
        scratch_shapes=[pltpu.VMEM((nj, SUBLANES, tn), F32),
                        pltpu.VMEM((nj, SUBLANES, tn), F32),
                        pltpu.VMEM((d, tn), BF16),
                        pltpu.VMEM((d, tn), BF16)],
        compiler_params=_params("arbitrary", "arbitrary"),
        name="upproj_conv_gate",
    )(h2, w_up, w_up, w_conv, w_conv)


def _downproj_kernel(g_ref, w_ref, gt_ref, gpost_ref, x1_hbm, o_hbm, acc_ref, x1_buf, o_buf,
                     x1_sem, o_sem, *, nk, k_last):
    i = pl.program_id(0)
    k = pl.program_id(1)
    tm = acc_ref.shape[0]

    @pl.when(k == 0)
    def _():
        _accumulate_dot(acc_ref, g_ref[...], w_ref, first=True)

    @pl.when((k > 0) & (k < nk - 1))
    def _():
        _accumulate_dot(acc_ref, g_ref[...], w_ref, first=False)

    @pl.when(k == nk - 1)
    def _():
        row0 = pl.multiple_of(i * tm, tm)

        def row_value(r):
            rows = slice(r * EPILOGUE_ROWS, (r + 1) * EPILOGUE_ROWS)
            return acc_ref[rows, :] + jnp.dot(g_ref[rows, 0:k_last], w_ref[0:k_last, :],
                                              preferred_element_type=F32)

        post_gain = gt_ref[0] * gpost_ref[...]

        def finish(slot, f):
            o_buf[slot] = x1_buf[slot] + (f * _rms_scale(f)) * post_gain

        _stream_epilogue(tm // EPILOGUE_ROWS,
                         [_row_copy(x1_hbm, x1_buf, x1_sem, row0, to_hbm=False)],
                         [_row_copy(o_hbm, o_buf, o_sem, row0, to_hbm=True)],
                         row_value, finish)


def _downproj(g, w_down, x1, gt, g_post, seq, tm=1024, tk=1024):
    t, d = x1.shape
    d_ff = g.shape[1]
    nk = pl.cdiv(d_ff, tk)
    k_last = d_ff - (nk - 1) * tk
    assert nk >= 3
    per_seq = seq // tm
    hbm = pl.BlockSpec(memory_space=pl.ANY)
    return pl.pallas_call(
        functools.partial(_downproj_kernel, nk=nk, k_last=k_last),
        out_shape=jax.ShapeDtypeStruct((t, d), F32),
        grid=(t // tm, nk),
        in_specs=[pl.BlockSpec((tm, tk), lambda i, k: (i, k)),
                  pl.BlockSpec((tk, d), lambda i, k: (k, 0)),
                  pl.BlockSpec((1, 1, d), lambda i, k: (i // per_seq, 0, 0)),
                  pl.BlockSpec((1, d), lambda i, k: (0, 0)),
                  hbm],
        out_specs=hbm,
        scratch_shapes=[pltpu.VMEM((tm, d), F32),
                        pltpu.VMEM((2, EPILOGUE_ROWS, d), F32),
                        pltpu.VMEM((2, EPILOGUE_ROWS, d), F32),
                        pltpu.SemaphoreType.DMA((2,)),
                        pltpu.SemaphoreType.DMA((2,))],
        compiler_params=_params("parallel", "arbitrary"),
        name="downproj_norm",
    )(g, w_down, gt, g_post.reshape(1, d), x1)


def kernel(x, c, positions, w_ada, b_ada, g_pre_mix, g_post_mix, w_in, lambda_q1, lambda_k1, lambda_q2, lambda_k2, g_subln, w_conv_mix, w_out, g_pre_ffn, g_post_ffn, w_up, w_conv_ffn, w_down):
    batch, seq, d = x.shape
    depth = w_ada.shape[0]
    assert depth == 1, "the attention lambda schedule below is written for a single layer"
    attn_width = ATTN_HEADS * 2 * ATTN_HEAD_DIM
    conv_width = d - attn_width
    d_ff = w_down.shape[1]
    t = batch * seq

    cos, sin = _rope_tables(positions)
    x2 = x.reshape(t, d)
    for l in range(depth):
        mod = _ada_modulation(c, w_ada[l], b_ada[l])
        sh1, sc1, gt1, sh2, sc2, gt2 = [m.reshape(batch, 1, d) for m in jnp.split(mod, N_MOD, axis=-1)]

        h = _prenorm(x2, g_pre_mix[l], sc1, sh1, seq)
        proj = _inproj(h, w_in[l], cos, sin, attn_width)
        attn = _diff_attention(proj, lambda_q1[l], lambda_k1[l], lambda_q2[l], lambda_k2[l],
                               g_subln[l], batch, seq)
        conv = _conv_mixer(proj, w_conv_mix[l], attn_width, conv_width, seq)
        x1, h2 = _outproj(attn, conv, w_out[l].astype(BF16), x2, gt1, g_post_mix[l], g_pre_ffn[l],
                          sc2, sh2, seq)
        gated = _upproj(h2, w_up[l], w_conv_ffn[l], d_ff, seq)
        x2 = _downproj(gated, w_down[l].astype(BF16), x1, gt2, g_post_ffn[l], seq)
    return x2.reshape(batch, seq, d)
```

```python
import functools
import math

import jax
import jax.numpy as jnp
from jax import lax
from jax.experimental import pallas as pl
from jax.experimental.pallas import tpu as pltpu

F32 = jnp.float32
BF16 = jnp.bfloat16

ATTN_HEADS = 8
ATTN_HEAD_DIM = 128
CONV_K = 3
ROPE_THETA = 10000.0
EPS = 1e-6
N_MOD = 6
LAM_INIT = 0.8 - 0.6 * math.exp(-0.3 * 0)

LANES = 128
SUBLANES = 8
BF16_ROWS = 16
VMEM_LIMIT = 56 * 1024 * 1024
ACC_COLS = 1024
EPILOGUE_ROWS = 128
DOT_ROWS = 256
ATTN_Q_COLS = 256


def _params(*sem):
    return pltpu.CompilerParams(dimension_semantics=sem, vmem_limit_bytes=VMEM_LIMIT)


def _rms_scale(v):
    return lax.rsqrt(jnp.mean(v * v, axis=-1, keepdims=True) + EPS)


def _ada_kernel(c_ref, w_ref, b_ref, o_ref):
    c = c_ref[...]
    cond = c * jax.nn.sigmoid(c)
    o_ref[...] = jnp.dot(cond.astype(BF16), w_ref[...].astype(BF16),
                         preferred_element_type=F32) + b_ref[...]


def _ada_modulation(c, w_ada, b_ada, tn=512):
    b, d = c.shape
    n = w_ada.shape[1]
    rows = -(-b // SUBLANES) * SUBLANES
    c_pad = jnp.pad(c, ((0, rows - b), (0, 0)))
    out = pl.pallas_call(
        _ada_kernel,
        out_shape=jax.ShapeDtypeStruct((rows, n), F32),
        grid=(n // tn,),
        in_specs=[pl.BlockSpec((rows, d), lambda j: (0, 0)),
                  pl.BlockSpec((d, tn), lambda j: (0, j)),
                  pl.BlockSpec((1, tn), lambda j: (0, j))],
        out_specs=pl.BlockSpec((rows, tn), lambda j: (0, j)),
        compiler_params=_params("parallel"),
        name="ada_modulation",
    )(c_pad, w_ada, b_ada.reshape(1, n))
    return out[:b]


def _rope_kernel(pos_ref, inv_ref, cos_ref, sin_ref):
    ang = pos_ref[...].astype(F32) * inv_ref[...]
    lane = lax.broadcasted_iota(jnp.int32, ang.shape, 1)
    sin = jnp.sin(ang)
    cos_ref[...] = jnp.cos(ang)
    sin_ref[...] = jnp.where(lane < ATTN_HEAD_DIM // 2, -sin, sin)


def _rope_tables(positions, tm=2048):
    t = positions.size
    dim = ATTN_HEAD_DIM
    inv_freq = ROPE_THETA ** (-jnp.arange(0, dim, 2, dtype=F32) / dim)
    inv = jnp.concatenate([inv_freq, inv_freq]).reshape(1, dim)
    return pl.pallas_call(
        _rope_kernel,
        out_shape=(jax.ShapeDtypeStruct((t, dim), F32), jax.ShapeDtypeStruct((t, dim), F32)),
        grid=(t // tm,),
        in_specs=[pl.BlockSpec((tm, 1), lambda i: (i, 0)),
                  pl.BlockSpec((1, dim), lambda i: (0, 0))],
        out_specs=(pl.BlockSpec((tm, dim), lambda i: (i, 0)),
                   pl.BlockSpec((tm, dim), lambda i: (i, 0))),
        compiler_params=_params("parallel"),
        name="rope_tables",
    )(positions.reshape(t, 1), inv)


def _prenorm_kernel(x_ref, g_ref, sc_ref, sh_ref, o_ref):
    x = x_ref[...]
    y = x * _rms_scale(x) * g_ref[...]
    o_ref[...] = (y * (1.0 + sc_ref[0]) + sh_ref[0]).astype(o_ref.dtype)


def _prenorm(x2, g, sc, sh, seq, tm=512):
    t, d = x2.shape
    per_seq = seq // tm
    return pl.pallas_call(
        _prenorm_kernel,
        out_shape=jax.ShapeDtypeStruct((t, d), BF16),
        grid=(t // tm,),
        in_specs=[pl.BlockSpec((tm, d), lambda i: (i, 0)),
                  pl.BlockSpec((1, d), lambda i: (0, 0)),
                  pl.BlockSpec((1, 1, d), lambda i: (i // per_seq, 0, 0)),
                  pl.BlockSpec((1, 1, d), lambda i: (i // per_seq, 0, 0))],
        out_specs=pl.BlockSpec((tm, d), lambda i: (i, 0)),
        compiler_params=_params("parallel"),
        name="prenorm",
    )(x2, g.reshape(1, d), sc, sh)


def _inproj_kernel(h_ref, w_ref, cos_ref, sin_ref, o_ref, *, n_q, n_qk, q_scale):
    j = pl.program_id(1)
    tm, tn = o_ref.shape

    @pl.when(j >= n_qk)
    def _():
        for r in range(tm // DOT_ROWS):
            rows = slice(r * DOT_ROWS, (r + 1) * DOT_ROWS)
            acc = jnp.dot(h_ref[rows, :], w_ref[...], preferred_element_type=F32)
            o_ref[rows, :] = acc.astype(o_ref.dtype)

    @pl.when(j < n_qk)
    def _():
        scale = jnp.where(j < n_q, q_scale, 1.0).astype(F32)
        for r in range(tm // DOT_ROWS):
            rows = slice(r * DOT_ROWS, (r + 1) * DOT_ROWS)
            acc = jnp.dot(h_ref[rows, :], w_ref[...], preferred_element_type=F32)
            cos = cos_ref[rows, :]
            sin = sin_ref[rows, :] * scale
            cos = cos * scale
            for c in range(tn // LANES):
                t = acc[:, c * LANES:(c + 1) * LANES]
                rot = t * cos + pltpu.roll(t, ATTN_HEAD_DIM // 2, 1) * sin
                o_ref[rows, c * LANES:(c + 1) * LANES] = rot.astype(o_ref.dtype)


def _inproj(h, w_in, cos, sin, attn_width, tm=1024, tn=1024):
    t, d = h.shape
    n = w_in.shape[1]
    kern = functools.partial(_inproj_kernel, n_q=attn_width // tn, n_qk=2 * attn_width // tn,
                             q_scale=ATTN_HEAD_DIM ** -0.5 * math.log2(math.e))
    return pl.pallas_call(
        kern,
        out_shape=jax.ShapeDtypeStruct((t, n), BF16),
        grid=(t // tm, n // tn),
        in_specs=[pl.BlockSpec((tm, d), lambda i, j: (i, 0)),
                  pl.BlockSpec((d, tn), lambda i, j: (0, j)),
                  pl.BlockSpec((tm, ATTN_HEAD_DIM), lambda i, j: (i, 0)),
                  pl.BlockSpec((tm, ATTN_HEAD_DIM), lambda i, j: (i, 0))],
        out_specs=pl.BlockSpec((tm, tn), lambda i, j: (i, j)),
        compiler_params=_params("parallel", "parallel"),
        name="inproj_rope",
    )(h, w_in, cos, sin)


def _attn_kernel(q_ref, k_ref, v_ref, lq1_ref, lk1_ref, lq2_ref, lk2_ref, g_ref,
                 o_ref, vt_ref, acc_ref, sa_ref, sb_ref):
    qi = pl.program_id(2)
    tq = q_ref.shape[0]
    tk = vt_ref.shape[2]
    d = ATTN_HEAD_DIM
    diag_chunks = tq // tk
    first_diag = qi * diag_chunks

    for t in range(diag_chunks):
        rows = pl.ds(pl.multiple_of((first_diag + t) * tk, tk), tk)
        vt_ref[first_diag + t] = v_ref[rows, :].T
    q_t = q_ref[...].T
    acc_ref[...] = jnp.zeros(acc_ref.shape, F32)

    chains = [(c, h) for c in range(2) for h in range(tq // ATTN_Q_COLS)]

    def live_chains(diag):
        if diag is None:
            return {n: False for n in range(len(chains))}
        return {n: h * ATTN_Q_COLS < (diag + 1) * tk
                for n, (c, h) in enumerate(chains) if (h + 1) * ATTN_Q_COLS > diag * tk}

    def score_dots(j, s_ref, diag=None):
        k = k_ref[pl.ds(pl.multiple_of(j * tk, tk), tk), :]
        for n in live_chains(diag):
            c, h = chains[n]
            cols = slice(h * ATTN_Q_COLS, (h + 1) * ATTN_Q_COLS)
            s_ref[n] = jnp.dot(k[:, c * d:(c + 1) * d], q_t[c * d:(c + 1) * d, cols],
                               preferred_element_type=F32)

    def fold(j, s_ref, m, l, diag=None):
        v_t = vt_ref[j]
        m, l = list(m), list(l)
        for n, masked in live_chains(diag).items():
            c, h = chains[n]
            cols = slice(h * ATTN_Q_COLS, (h + 1) * ATTN_Q_COLS)
            s = s_ref[n]
            if masked:
                kv = lax.broadcasted_iota(jnp.int32, s.shape, 0) + diag * tk
                qq = lax.broadcasted_iota(jnp.int32, s.shape, 1) + h * ATTN_Q_COLS
                s = jnp.where(kv <= qq, s, jnp.finfo(F32).min)
            m_new = jnp.maximum(m[n], jnp.max(s, axis=0, keepdims=True))
            alpha = jnp.exp2(m[n] - m_new)
            p = jnp.exp2(s - m_new)
            l[n] = alpha * l[n] + jnp.sum(p, axis=0, keepdims=True)
            acc_ref[c, :, cols] = alpha * acc_ref[c, :, cols] + jnp.dot(
                v_t, p.astype(BF16), preferred_element_type=F32)
            m[n] = m_new
        return tuple(m), tuple(l)

    neg = jnp.full((1, ATTN_Q_COLS), -jnp.inf, F32)
    zero = jnp.zeros((1, ATTN_Q_COLS), F32)

    def pair(i, ml):
        m, l = ml
        score_dots(2 * i + 1, sb_ref)
        m, l = fold(2 * i, sa_ref, m, l)
        score_dots(2 * i + 2, sa_ref)
        return fold(2 * i + 1, sb_ref, m, l)

    score_dots(0, sa_ref)
    m, l = lax.fori_loop(0, first_diag // 2, pair,
                         ((neg,) * len(chains), (zero,) * len(chains)))
    for t in range(diag_chunks):
        cur, nxt = (sa_ref, sb_ref) if t % 2 == 0 else (sb_ref, sa_ref)
        if t + 1 < diag_chunks:
            score_dots(first_diag + t + 1, nxt, diag=t + 1)
        m, l = fold(first_diag + t, cur, m, l, diag=t)
    l = [jnp.concatenate([l[n] for n, (c, h) in enumerate(chains) if c == cc], axis=1)
         for cc in range(2)]

    lam = (jnp.exp(jnp.sum(lq1_ref[...] * lk1_ref[...], axis=-1, keepdims=True))
           - jnp.exp(jnp.sum(lq2_ref[...] * lk2_ref[...], axis=-1, keepdims=True)) + LAM_INIT)
    o_t = acc_ref[0] * (1.0 / l[0]) - acc_ref[1] * (lam / l[1])
    o_t = o_t * lax.rsqrt(jnp.mean(o_t * o_t, axis=0, keepdims=True) + EPS)
    o_ref[...] = (o_t.T * (g_ref[...] * (1.0 - LAM_INIT))).astype(o_ref.dtype)


def _diff_attention(proj, lq1, lk1, lq2, lk2, g_subln, batch, seq, tq=1024, tk=512):
    t = proj.shape[0]
    hw = 2 * ATTN_HEAD_DIM
    nq = seq // tq
    assert tq % (2 * tk) == 0 and tk % ATTN_Q_COLS == 0
    n_chains = 2 * tq // ATTN_Q_COLS
    k_off = ATTN_HEADS
    v_off = 2 * ATTN_HEADS
    vec = lambda a: a.reshape(1, -1)
    const = lambda b, h, i: (0, 0)
    return pl.pallas_call(
        _attn_kernel,
        out_shape=jax.ShapeDtypeStruct((t, ATTN_HEADS * hw), BF16),
        grid=(batch, ATTN_HEADS, nq),
        in_specs=[pl.BlockSpec((tq, hw), lambda b, h, i: (b * nq + i, h)),
                  pl.BlockSpec((seq, hw), lambda b, h, i: (b, k_off + h)),
                  pl.BlockSpec((seq, hw), lambda b, h, i: (b, v_off + h)),
                  pl.BlockSpec((1, ATTN_HEAD_DIM), const),
                  pl.BlockSpec((1, ATTN_HEAD_DIM), const),
                  pl.BlockSpec((1, ATTN_HEAD_DIM), const),
                  pl.BlockSpec((1, ATTN_HEAD_DIM), const),
                  pl.BlockSpec((1, hw), const)],
        out_specs=pl.BlockSpec((tq, hw), lambda b, h, i: (b * nq + i, h)),
        scratch_shapes=[pltpu.VMEM((seq // tk, hw, tk), BF16),
                        pltpu.VMEM((2, hw, tq), F32),
                        pltpu.VMEM((n_chains, tk, ATTN_Q_COLS), F32),
                        pltpu.VMEM((n_chains, tk, ATTN_Q_COLS), F32)],
        compiler_params=_params("parallel", "parallel", "arbitrary"),
        name="diff_attention",
    )(proj, proj, proj, vec(lq1), vec(lk1), vec(lq2), vec(lk2), vec(g_subln))


def _conv3(p, tail, w):
    w0, w1, w2 = w[0:1], w[1:2], w[2:3]
    body = pltpu.roll(p, 2, 0) * w0 + pltpu.roll(p, 1, 0) * w1 + p * w2
    first = p[0:SUBLANES]
    row = lax.broadcasted_iota(jnp.int32, first.shape, 0)
    h1 = jnp.where(row < 1, pltpu.roll(tail, 1, 0), pltpu.roll(first, 1, 0))
    h2 = jnp.where(row < 2, pltpu.roll(tail, 2, 0), pltpu.roll(first, 2, 0))
    head = jnp.concatenate([h2 * w0 + h1 * w1 + first * w2, body[SUBLANES:BF16_ROWS]], axis=0)
    return body, head


def _convmix_kernel(gb_ref, gc_ref, hc_ref, gct_ref, hct_ref, w_ref, o_ref, *, per_seq):
    i = pl.program_id(0)
    p = gc_ref[...].astype(F32) * hc_ref[...].astype(F32)
    tail = (gct_ref[...].astype(F32) * hct_ref[...].astype(F32))[BF16_ROWS - SUBLANES:]
    tail = jnp.where(i % per_seq == 0, 0.0, tail)
    body, head = _conv3(p, tail, w_ref[...])
    gb = gb_ref[...].astype(F32)
    o_ref[...] = (gb * body).astype(o_ref.dtype)
    o_ref[0:BF16_ROWS, :] = (gb[0:BF16_ROWS] * head).astype(o_ref.dtype)


def _conv_mixer(proj, w_conv, attn_width, conv_width, seq, tm=1024, tc=1024):
    t = proj.shape[0]
    per_seq = seq // tm
    cb = lambda off: (off // tc)
    b_off, c_off, h_off = cb(3 * attn_width), cb(3 * attn_width + conv_width), cb(3 * attn_width + 2 * conv_width)
    tail_blocks = tm // BF16_ROWS
    tail_idx = lambda i: jnp.maximum(i * tail_blocks - 1, 0)
    return pl.pallas_call(
        functools.partial(_convmix_kernel, per_seq=per_seq),
        out_shape=jax.ShapeDtypeStruct((t, conv_width), BF16),
        grid=(t // tm, conv_width // tc),
        in_specs=[pl.BlockSpec((tm, tc), lambda i, j: (i, b_off + j)),
                  pl.BlockSpec((tm, tc), lambda i, j: (i, c_off + j)),
                  pl.BlockSpec((tm, tc), lambda i, j: (i, h_off + j)),
                  pl.BlockSpec((BF16_ROWS, tc), lambda i, j: (tail_idx(i), c_off + j)),
                  pl.BlockSpec((BF16_ROWS, tc), lambda i, j: (tail_idx(i), h_off + j)),
                  pl.BlockSpec((CONV_K, tc), lambda i, j: (0, j))],
        out_specs=pl.BlockSpec((tm, tc), lambda i, j: (i, j)),
        compiler_params=_params("parallel", "parallel"),
        name="conv_mixer",
    )(proj, proj, proj, proj, proj, w_conv)


def _accumulate_dot(acc_ref, a, w_ref, first, k_rows=None):
    n = acc_ref.shape[1]
    for c in range(n // ACC_COLS):
        cols = slice(c * ACC_COLS, (c + 1) * ACC_COLS)
        w = w_ref[:, cols] if k_rows is None else w_ref[0:k_rows, cols]
        part = jnp.dot(a, w, preferred_element_type=F32)
        if first:
            acc_ref[:, cols] = part
        else:
            acc_ref[:, cols] += part


def _stream_epilogue(n_chunks, loads, stores, finish_rows, finish):
    for load in loads:
        load(0, 0).start()
    stats = None
    for r in range(n_chunks + 1):
        new_stats = finish_rows(r) if r < n_chunks else None
        if r >= 1:
            done, slot = r - 1, (r - 1) % 2
            for load in loads:
                load(done, slot).wait()
            if done >= 2:
                for store in stores:
                    store(done - 2, slot).wait()
            finish(done, slot, stats)
            for store in stores:
                store(done, slot).start()
        if r + 1 < n_chunks:
            for load in loads:
                load(r + 1, (r + 1) % 2).start()
        stats = new_stats
    for r in range(max(n_chunks - 2, 0), n_chunks):
        for store in stores:
            store(r, r % 2).wait()


def _row_copy(hbm_ref, buf_ref, sem_ref, row0, to_hbm):
    rows_per = buf_ref.shape[1]

    def make(chunk, slot):
        hbm = hbm_ref.at[pl.ds(row0 + chunk * rows_per, rows_per), :]
        if to_hbm:
            return pltpu.make_async_copy(buf_ref.at[slot], hbm, sem_ref.at[slot])
        return pltpu.make_async_copy(hbm, buf_ref.at[slot], sem_ref.at[slot])
    return make


def _outproj_kernel(a_ref, c_ref, w_ref, gt_ref, gpost_ref, gpre_ref, sc_ref, sh_ref, x_hbm,
                    x1_hbm, h2_hbm, acc_ref, x_buf, x1_buf, h2_buf, x_sem, x1_sem, h2_sem,
                    *, nk_attn):
    i = pl.program_id(0)
    k = pl.program_id(1)
    nk = pl.num_programs(1)
    tm = acc_ref.shape[0]

    @pl.when(k == 0)
    def _():
        _accumulate_dot(acc_ref, a_ref[...], w_ref, first=True)

    @pl.when((k > 0) & (k < nk_attn))
    def _():
        _accumulate_dot(acc_ref, a_ref[...], w_ref, first=False)

    @pl.when((k >= nk_attn) & (k < nk - 1))
    def _():
        _accumulate_dot(acc_ref, c_ref[...], w_ref, first=False)

    @pl.when(k == nk - 1)
    def _():
        row0 = pl.multiple_of(i * tm, tm)

        def finish_rows(r):
            rows = slice(r * EPILOGUE_ROWS, (r + 1) * EPILOGUE_ROWS)
            y = acc_ref[rows, :] + jnp.dot(c_ref[rows, :], w_ref[...],
                                           preferred_element_type=F32)
            acc_ref[rows, :] = y
            return _rms_scale(y)

        post_gain = gt_ref[0] * gpost_ref[...]
        pre_gain = gpre_ref[...] * (1.0 + sc_ref[0])
        shift = sh_ref[0]

        def finish(r, slot, y_scale):
            rows = slice(r * EPILOGUE_ROWS, (r + 1) * EPILOGUE_ROWS)
            x1 = x_buf[slot] + (acc_ref[rows, :] * y_scale) * post_gain
            x1_buf[slot] = x1
            h2_buf[slot] = ((x1 * _rms_scale(x1)) * pre_gain + shift).astype(h2_buf.dtype)

        _stream_epilogue(tm // EPILOGUE_ROWS,
                         [_row_copy(x_hbm, x_buf, x_sem, row0, to_hbm=False)],
                         [_row_copy(x1_hbm, x1_buf, x1_sem, row0, to_hbm=True),
                          _row_copy(h2_hbm, h2_buf, h2_sem, row0, to_hbm=True)],
                         finish_rows, finish)


def _outproj(attn, conv, w_out, x2, gt, g_post, g_pre, sc, sh, seq, tm=1024, tk=512):
    t, d = x2.shape
    ka = attn.shape[1]
    nk_attn = ka // tk
    nk = d // tk
    assert nk - 1 > nk_attn >= 1
    per_seq = seq // tm
    row = lambda i, k: (0, 0)
    mod = lambda i, k: (i // per_seq, 0, 0)
    hbm = pl.BlockSpec(memory_space=pl.ANY)
    return pl.pallas_call(
        functools.partial(_outproj_kernel, nk_attn=nk_attn),
        out_shape=(jax.ShapeDtypeStruct((t, d), F32), jax.ShapeDtypeStruct((t, d), BF16)),
        grid=(t // tm, nk),
        in_specs=[pl.BlockSpec((tm, tk), lambda i, k: (i, jnp.minimum(k, nk_attn - 1))),
                  pl.BlockSpec((tm, tk), lambda i, k: (i, jnp.maximum(k - nk_attn, 0))),
                  pl.BlockSpec((tk, d), lambda i, k: (k, 0)),
                  pl.BlockSpec((1, 1, d), mod),
                  pl.BlockSpec((1, d), row),
                  pl.BlockSpec((1, d), row),
                  pl.BlockSpec((1, 1, d), mod),
                  pl.BlockSpec((1, 1, d), mod),
                  hbm],
        out_specs=(hbm, hbm),
        scratch_shapes=[pltpu.VMEM((tm, d), F32),
                        pltpu.VMEM((2, EPILOGUE_ROWS, d), F32),
                        pltpu.VMEM((2, EPILOGUE_ROWS, d), F32),
                        pltpu.VMEM((2, EPILOGUE_ROWS, d), BF16),
                        pltpu.SemaphoreType.DMA((2,)),
                        pltpu.SemaphoreType.DMA((2,)),
                        pltpu.SemaphoreType.DMA((2,))],
        compiler_params=_params("parallel", "arbitrary"),
        name="outproj_norms",
    )(attn, conv, w_out, gt, g_post.reshape(1, d), g_pre.reshape(1, d), sc, sh, x2)


def _upproj_kernel(h_ref, wg32_ref, wv32_ref, cg_ref, cv_ref, o_ref, tail_g, tail_v, wg_ref, wv_ref,
                   *, per_seq):
    i = pl.program_id(0)
    j = pl.program_id(1)
    wg_ref[...] = wg32_ref[...].astype(BF16)
    wv_ref[...] = wv32_ref[...].astype(BF16)

    @pl.when(i % per_seq == 0)
    def _():
        tail_g[j] = jnp.zeros(tail_g.shape[1:], F32)
        tail_v[j] = jnp.zeros(tail_v.shape[1:], F32)

    tg = tail_g[j]
    tv = tail_v[j]
    cg = cg_ref[...]
    cv = cv_ref[...]
    for r in range(o_ref.shape[0] // DOT_ROWS):
        rows = slice(r * DOT_ROWS, (r + 1) * DOT_ROWS)
        h = h_ref[rows, :]
        ug = jnp.dot(h, wg_ref[...], preferred_element_type=F32)
        uv = jnp.dot(h, wv_ref[...], preferred_element_type=F32)
        g_body, g_head = _conv3(ug, tg, cg)
        v_body, v_head = _conv3(uv, tv, cv)
        tg = ug[DOT_ROWS - SUBLANES:]
        tv = uv[DOT_ROWS - SUBLANES:]
        o_ref[rows, :] = (g_body * jax.nn.sigmoid(g_body) * v_body).astype(o_ref.dtype)
        head = slice(r * DOT_ROWS, r * DOT_ROWS + BF16_ROWS)
        o_ref[head, :] = (g_head * jax.nn.sigmoid(g_head) * v_head).astype(o_ref.dtype)
    tail_g[j] = tg
    tail_v[j] = tv


def _upproj(h2, w_up, w_conv, d_ff, seq, tm=2048, tn=256):
    t, d = h2.shape
    nj = d_ff // tn
    per_seq = seq // tm
    return pl.pallas_call(
        functools.partial(_upproj_kernel, per_seq=per_seq),
        out_shape=jax.ShapeDtypeStruct((t, d_ff), BF16),
        grid=(t // tm, nj),
        in_specs=[pl.BlockSpec((tm, d), lambda i, j: (i, 0), pipeline_mode=pl.Buffered(1)),
                  pl.BlockSpec((d, tn), lambda i, j: (0, j)),
                  pl.BlockSpec((d, tn), lambda i, j: (0, nj + j)),
                  pl.BlockSpec((CONV_K, tn), lambda i, j: (0, j)),
                  pl.BlockSpec((CONV_K, tn), lambda i, j: (0, nj + j))],
        out_specs=pl.BlockSpec((tm, tn), lambda i, j: (i, j)),
        scratch_shapes=[pltpu.VMEM((nj, SUBLANES, tn), F32),
                        pltpu.VMEM((nj, SUBLANES, tn), F32),
                        pltpu.VMEM((d, tn), BF16),
                        pltpu.VMEM((d, tn), BF16)],
        compiler_params=_params("arbitrary", "arbitrary"),
        name="upproj_conv_gate",
    )(h2, w_up, w_up, w_conv, w_conv)


def _downproj_kernel(g_ref, w_ref, gt_ref, gpost_ref, x1_hbm, o_hbm, acc_ref, x1_buf, o_buf,
                     x1_sem, o_sem, *, nk, k_last):
    i = pl.program_id(0)
    k = pl.program_id(1)
    tm = acc_ref.shape[0]

    @pl.when(k == 0)
    def _():
        _accumulate_dot(acc_ref, g_ref[...], w_ref, first=True)

    @pl.when((k > 0) & (k < nk - 1))
    def _():
        _accumulate_dot(acc_ref, g_ref[...], w_ref, first=False)

    @pl.when(k == nk - 1)
    def _():
        row0 = pl.multiple_of(i * tm, tm)

        def finish_rows(r):
            rows = slice(r * EPILOGUE_ROWS, (r + 1) * EPILOGUE_ROWS)
            f = acc_ref[rows, :] + jnp.dot(g_ref[rows, 0:k_last], w_ref[0:k_last, :],
                                           preferred_element_type=F32)
            acc_ref[rows, :] = f
            return _rms_scale(f)

        post_gain = gt_ref[0] * gpost_ref[...]

        def finish(r, slot, f_scale):
            rows = slice(r * EPILOGUE_ROWS, (r + 1) * EPILOGUE_ROWS)
            o_buf[slot] = x1_buf[slot] + (acc_ref[rows, :] * f_scale) * post_gain

        _stream_epilogue(tm // EPILOGUE_ROWS,
                         [_row_copy(x1_hbm, x1_buf, x1_sem, row0, to_hbm=False)],
                         [_row_copy(o_hbm, o_buf, o_sem, row0, to_hbm=True)],
                         finish_rows, finish)


def _downproj(g, w_down, x1, gt, g_post, seq, tm=1024, tk=1024):
    t, d = x1.shape
    d_ff = g.shape[1]
    nk = pl.cdiv(d_ff, tk)
    k_last = d_ff - (nk - 1) * tk
    assert nk >= 3
    per_seq = seq // tm
    hbm = pl.BlockSpec(memory_space=pl.ANY)
    return pl.pallas_call(
        functools.partial(_downproj_kernel, nk=nk, k_last=k_last),
        out_shape=jax.ShapeDtypeStruct((t, d), F32),
        grid=(t // tm, nk),
        in_specs=[pl.BlockSpec((tm, tk), lambda i, k: (i, k)),
                  pl.BlockSpec((tk, d), lambda i, k: (k, 0)),
                  pl.BlockSpec((1, 1, d), lambda i, k: (i // per_seq, 0, 0)),
                  pl.BlockSpec((1, d), lambda i, k: (0, 0)),
                  hbm],
        out_specs=hbm,
        scratch_shapes=[pltpu.VMEM((tm, d), F32),
                        pltpu.VMEM((2, EPILOGUE_ROWS, d), F32),
                        pltpu.VMEM((2, EPILOGUE_ROWS, d), F32),
                        pltpu.SemaphoreType.DMA((2,)),
                        pltpu.SemaphoreType.DMA((2,))],
        compiler_params=_params("parallel", "arbitrary"),
        name="downproj_norm",
    )(g, w_down, gt, g_post.reshape(1, d), x1)


def kernel(x, c, positions, w_ada, b_ada, g_pre_mix, g_post_mix, w_in, lambda_q1, lambda_k1, lambda_q2, lambda_k2, g_subln, w_conv_mix, w_out, g_pre_ffn, g_post_ffn, w_up, w_conv_ffn, w_down):
    batch, seq, d = x.shape
    depth = w_ada.shape[0]
    assert depth == 1, "the attention lambda schedule below is written for a single layer"
    attn_width = ATTN_HEADS * 2 * ATTN_HEAD_DIM
    conv_width = d - attn_width
    d_ff = w_down.shape[1]
    t = batch * seq

    cos, sin = _rope_tables(positions)
    x2 = x.reshape(t, d)
    for l in range(depth):
        mod = _ada_modulation(c, w_ada[l], b_ada[l])
        sh1, sc1, gt1, sh2, sc2, gt2 = [m.reshape(batch, 1, d) for m in jnp.split(mod, N_MOD, axis=-1)]

        h = _prenorm(x2, g_pre_mix[l], sc1, sh1, seq)
        proj = _inproj(h, w_in[l].astype(BF16), cos, sin, attn_width)
        attn = _diff_attention(proj, lambda_q1[l], lambda_k1[l], lambda_q2[l], lambda_k2[l],
                               g_subln[l], batch, seq)
        conv = _conv_mixer(proj, w_conv_mix[l], attn_width, conv_width, seq)
        x1, h2 = _outproj(attn, conv, w_out[l].astype(BF16), x2, gt1, g_post_mix[l], g_pre_ffn[l],
                          sc2, sh2, seq)
        gated = _upproj(h2, w_up[l], w_conv_ffn[l], d_ff, seq)
        x2 = _downproj(gated, w_down[l].astype(BF16), x1, gt2, g_post_ffn[l], seq)
    return x2.reshape(batch, seq, d)
```

```python
import functools
import math

import jax
import jax.numpy as jnp
from jax import lax
from jax.experimental import pallas as pl
from jax.experimental.pallas import tpu as pltpu

F32 = jnp.float32
BF16 = jnp.bfloat16

ATTN_HEADS = 8
ATTN_HEAD_DIM = 128
CONV_K = 3
ROPE_THETA = 10000.0
EPS = 1e-6
N_MOD = 6
LAM_INIT = 0.8 - 0.6 * math.exp(-0.3 * 0)

LANES = 128
SUBLANES = 8
BF16_ROWS = 16
VMEM_LIMIT = 56 * 1024 * 1024
ACC_COLS = 1024
EPILOGUE_ROWS = 128
DOT_ROWS = 256
ATTN_Q_COLS = 256
UP_COLS = 256
IN_COLS = 1024


def _params(*sem):
    return pltpu.CompilerParams(dimension_semantics=sem, vmem_limit_bytes=VMEM_LIMIT)


def _rms_scale(v):
    return lax.rsqrt(jnp.mean(v * v, axis=-1, keepdims=True) + EPS)


def _ada_kernel(c_ref, w_ref, b_ref, o_ref):
    c = c_ref[...]
    cond = c * jax.nn.sigmoid(c)
    o_ref[...] = jnp.dot(cond.astype(BF16), w_ref[...].astype(BF16),
                         preferred_element_type=F32) + b_ref[...]


def _ada_modulation(c, w_ada, b_ada, tn=512):
    b, d = c.shape
    n = w_ada.shape[1]
    rows = -(-b // SUBLANES) * SUBLANES
    c_pad = jnp.pad(c, ((0, rows - b), (0, 0)))
    out = pl.pallas_call(
        _ada_kernel,
        out_shape=jax.ShapeDtypeStruct((rows, n), F32),
        grid=(n // tn,),
        in_specs=[pl.BlockSpec((rows, d), lambda j: (0, 0)),
                  pl.BlockSpec((d, tn), lambda j: (0, j)),
                  pl.BlockSpec((1, tn), lambda j: (0, j))],
        out_specs=pl.BlockSpec((rows, tn), lambda j: (0, j)),
        compiler_params=_params("parallel"),
        name="ada_modulation",
    )(c_pad, w_ada, b_ada.reshape(1, n))
    return out[:b]


def _rope_kernel(pos_ref, inv_ref, cos_ref, sin_ref):
    ang = pos_ref[...].astype(F32) * inv_ref[...]
    lane = lax.broadcasted_iota(jnp.int32, ang.shape, 1)
    sin = jnp.sin(ang)
    cos_ref[...] = jnp.cos(ang)
    sin_ref[...] = jnp.where(lane < ATTN_HEAD_DIM // 2, -sin, sin)


def _rope_tables(positions, tm=2048):
    t = positions.size
    dim = ATTN_HEAD_DIM
    inv_freq = ROPE_THETA ** (-jnp.arange(0, dim, 2, dtype=F32) / dim)
    inv = jnp.concatenate([inv_freq, inv_freq]).reshape(1, dim)
    return pl.pallas_call(
        _rope_kernel,
        out_shape=(jax.ShapeDtypeStruct((t, dim), F32), jax.ShapeDtypeStruct((t, dim), F32)),
        grid=(t // tm,),
        in_specs=[pl.BlockSpec((tm, 1), lambda i: (i, 0)),
                  pl.BlockSpec((1, dim), lambda i: (0, 0))],
        out_specs=(pl.BlockSpec((tm, dim), lambda i: (i, 0)),
                   pl.BlockSpec((tm, dim), lambda i: (i, 0))),
        compiler_params=_params("parallel"),
        name="rope_tables",
    )(positions.reshape(t, 1), inv)


def _prenorm_kernel(x_ref, g_ref, sc_ref, sh_ref, o_ref):
    x = x_ref[...]
    y = x * _rms_scale(x) * g_ref[...]
    o_ref[...] = (y * (1.0 + sc_ref[0]) + sh_ref[0]).astype(o_ref.dtype)


def _prenorm(x2, g, sc, sh, seq, tm=512):
    t, d = x2.shape
    per_seq = seq // tm
    return pl.pallas_call(
        _prenorm_kernel,
        out_shape=jax.ShapeDtypeStruct((t, d), BF16),
        grid=(t // tm,),
        in_specs=[pl.BlockSpec((tm, d), lambda i: (i, 0)),
                  pl.BlockSpec((1, d), lambda i: (0, 0)),
                  pl.BlockSpec((1, 1, d), lambda i: (i // per_seq, 0, 0)),
                  pl.BlockSpec((1, 1, d), lambda i: (i // per_seq, 0, 0))],
        out_specs=pl.BlockSpec((tm, d), lambda i: (i, 0)),
        compiler_params=_params("parallel"),
        name="prenorm",
    )(x2, g.reshape(1, d), sc, sh)


def _inproj_kernel(h_ref, w_ref, cos_ref, sin_ref, o_ref, *, n_q, n_qk, q_scale):
    j = pl.program_id(1)
    tm, tn = o_ref.shape

    @pl.when(j >= n_qk)
    def _():
        for r in range(tm // DOT_ROWS):
            rows = slice(r * DOT_ROWS, (r + 1) * DOT_ROWS)
            acc = jnp.dot(h_ref[rows, :], w_ref[...], preferred_element_type=F32)
            o_ref[rows, :] = acc.astype(o_ref.dtype)

    @pl.when(j < n_qk)
    def _():
        scale = jnp.where(j < n_q, q_scale, 1.0).astype(F32)
        for r in range(tm // DOT_ROWS):
            rows = slice(r * DOT_ROWS, (r + 1) * DOT_ROWS)
            acc = jnp.dot(h_ref[rows, :], w_ref[...], preferred_element_type=F32)
            cos = cos_ref[rows, :]
            sin = sin_ref[rows, :] * scale
            cos = cos * scale
            for c in range(tn // LANES):
                t = acc[:, c * LANES:(c + 1) * LANES]
                rot = t * cos + pltpu.roll(t, ATTN_HEAD_DIM // 2, 1) * sin
                o_ref[rows, c * LANES:(c + 1) * LANES] = rot.astype(o_ref.dtype)


def _inproj(h, w_in, cos, sin, attn_width, tm=1024):
    t, d = h.shape
    tn = w_in.shape[2]
    n = w_in.shape[0] * tn
    kern = functools.partial(_inproj_kernel, n_q=attn_width // tn, n_qk=2 * attn_width // tn,
                             q_scale=ATTN_HEAD_DIM ** -0.5 * math.log2(math.e))
    return pl.pallas_call(
        kern,
        out_shape=jax.ShapeDtypeStruct((t, n), BF16),
        grid=(t // tm, n // tn),
        in_specs=[pl.BlockSpec((tm, d), lambda i, j: (i, 0)),
                  pl.BlockSpec((None, d, tn), lambda i, j: (j, 0, 0)),
                  pl.BlockSpec((tm, ATTN_HEAD_DIM), lambda i, j: (i, 0)),
                  pl.BlockSpec((tm, ATTN_HEAD_DIM), lambda i, j: (i, 0))],
        out_specs=pl.BlockSpec((tm, tn), lambda i, j: (i, j)),
        compiler_params=_params("parallel", "parallel"),
        name="inproj_rope",
    )(h, w_in, cos, sin)


def _attn_kernel(q_ref, k_ref, v_ref, lq1_ref, lk1_ref, lq2_ref, lk2_ref, g_ref,
                 o_ref, vt_ref, acc_ref, sa_ref, sb_ref):
    qi = pl.program_id(2)
    tq = q_ref.shape[0]
    tk = vt_ref.shape[2]
    d = ATTN_HEAD_DIM
    diag_chunks = tq // tk
    first_diag = qi * diag_chunks

    for t in range(diag_chunks):
        rows = pl.ds(pl.multiple_of((first_diag + t) * tk, tk), tk)
        vt_ref[first_diag + t] = v_ref[rows, :].T
    q_t = q_ref[...].T
    acc_ref[...] = jnp.zeros(acc_ref.shape, F32)

    chains = [(c, h) for c in range(2) for h in range(tq // ATTN_Q_COLS)]

    def live_chains(diag):
        if diag is None:
            return {n: False for n in range(len(chains))}
        return {n: h * ATTN_Q_COLS < (diag + 1) * tk
                for n, (c, h) in enumerate(chains) if (h + 1) * ATTN_Q_COLS > diag * tk}

    def score_dots(j, s_ref, diag=None):
        k = k_ref[pl.ds(pl.multiple_of(j * tk, tk), tk), :]
        for n in live_chains(diag):
            c, h = chains[n]
            cols = slice(h * ATTN_Q_COLS, (h + 1) * ATTN_Q_COLS)
            s_ref[n] = jnp.dot(k[:, c * d:(c + 1) * d], q_t[c * d:(c + 1) * d, cols],
                               preferred_element_type=F32)

    def fold(j, s_ref, m, l, diag=None):
        v_t = vt_ref[j]
        m, l = list(m), list(l)
        for n, masked in live_chains(diag).items():
            c, h = chains[n]
            cols = slice(h * ATTN_Q_COLS, (h + 1) * ATTN_Q_COLS)
            s = s_ref[n]
            if masked:
                kv = lax.broadcasted_iota(jnp.int32, s.shape, 0) + diag * tk
                qq = lax.broadcasted_iota(jnp.int32, s.shape, 1) + h * ATTN_Q_COLS
                s = jnp.where(kv <= qq, s, jnp.finfo(F32).min)
            m_new = jnp.maximum(m[n], jnp.max(s, axis=0, keepdims=True))
            alpha = jnp.exp2(m[n] - m_new)
            p = jnp.exp2(s - m_new)
            l[n] = alpha * l[n] + jnp.sum(p, axis=0, keepdims=True)
            acc_ref[c, :, cols] = alpha * acc_ref[c, :, cols] + jnp.dot(
                v_t, p.astype(BF16), preferred_element_type=F32)
            m[n] = m_new
        return tuple(m), tuple(l)

    neg = jnp.full((1, ATTN_Q_COLS), -jnp.inf, F32)
    zero = jnp.zeros((1, ATTN_Q_COLS), F32)

    def pair(i, ml):
        m, l = ml
        score_dots(2 * i + 1, sb_ref)
        m, l = fold(2 * i, sa_ref, m, l)
        score_dots(2 * i + 2, sa_ref)
        return fold(2 * i + 1, sb_ref, m, l)

    score_dots(0, sa_ref)
    m, l = lax.fori_loop(0, first_diag // 2, pair,
                         ((neg,) * len(chains), (zero,) * len(chains)))
    for t in range(diag_chunks):
        cur, nxt = (sa_ref, sb_ref) if t % 2 == 0 else (sb_ref, sa_ref)
        if t + 1 < diag_chunks:
            score_dots(first_diag + t + 1, nxt, diag=t + 1)
        m, l = fold(first_diag + t, cur, m, l, diag=t)
    l = [jnp.concatenate([l[n] for n, (c, h) in enumerate(chains) if c == cc], axis=1)
         for cc in range(2)]

    lam = (jnp.exp(jnp.sum(lq1_ref[...] * lk1_ref[...], axis=-1, keepdims=True))
           - jnp.exp(jnp.sum(lq2_ref[...] * lk2_ref[...], axis=-1, keepdims=True)) + LAM_INIT)
    o_t = acc_ref[0] * (1.0 / l[0]) - acc_ref[1] * (lam / l[1])
    o_t = o_t * lax.rsqrt(jnp.mean(o_t * o_t, axis=0, keepdims=True) + EPS)
    o_ref[...] = (o_t.T * (g_ref[...] * (1.0 - LAM_INIT))).astype(o_ref.dtype)


def _diff_attention(proj, lq1, lk1, lq2, lk2, g_subln, batch, seq, tq=1024, tk=512):
    t = proj.shape[0]
    hw = 2 * ATTN_HEAD_DIM
    nq = seq // tq
    assert tq % (2 * tk) == 0 and tk % ATTN_Q_COLS == 0
    n_chains = 2 * tq // ATTN_Q_COLS
    k_off = ATTN_HEADS
    v_off = 2 * ATTN_HEADS
    vec = lambda a: a.reshape(1, -1)
    const = lambda b, h, i: (0, 0)
    return pl.pallas_call(
        _attn_kernel,
        out_shape=jax.ShapeDtypeStruct((t, ATTN_HEADS * hw), BF16),
        grid=(batch, ATTN_HEADS, nq),
        in_specs=[pl.BlockSpec((tq, hw), lambda b, h, i: (b * nq + i, h)),
                  pl.BlockSpec((seq, hw), lambda b, h, i: (b, k_off + h)),
                  pl.BlockSpec((seq, hw), lambda b, h, i: (b, v_off + h)),
                  pl.BlockSpec((1, ATTN_HEAD_DIM), const),
                  pl.BlockSpec((1, ATTN_HEAD_DIM), const),
                  pl.BlockSpec((1, ATTN_HEAD_DIM), const),
                  pl.BlockSpec((1, ATTN_HEAD_DIM), const),
                  pl.BlockSpec((1, hw), const)],
        out_specs=pl.BlockSpec((tq, hw), lambda b, h, i: (b * nq + i, h)),
        scratch_shapes=[pltpu.VMEM((seq // tk, hw, tk), BF16),
                        pltpu.VMEM((2, hw, tq), F32),
                        pltpu.VMEM((n_chains, tk, ATTN_Q_COLS), F32),
                        pltpu.VMEM((n_chains, tk, ATTN_Q_COLS), F32)],
        compiler_params=_params("parallel", "parallel", "arbitrary"),
        name="diff_attention",
    )(proj, proj, proj, vec(lq1), vec(lk1), vec(lq2), vec(lk2), vec(g_subln))


def _conv3(p, tail, w):
    w0, w1, w2 = w[0:1], w[1:2], w[2:3]
    body = pltpu.roll(p, 2, 0) * w0 + pltpu.roll(p, 1, 0) * w1 + p * w2
    first = p[0:SUBLANES]
    row = lax.broadcasted_iota(jnp.int32, first.shape, 0)
    h1 = jnp.where(row < 1, pltpu.roll(tail, 1, 0), pltpu.roll(first, 1, 0))
    h2 = jnp.where(row < 2, pltpu.roll(tail, 2, 0), pltpu.roll(first, 2, 0))
    head = jnp.concatenate([h2 * w0 + h1 * w1 + first * w2, body[SUBLANES:BF16_ROWS]], axis=0)
    return body, head


def _convmix_kernel(gb_ref, gc_ref, hc_ref, gct_ref, hct_ref, w_ref, o_ref, *, per_seq):
    i = pl.program_id(0)
    p = gc_ref[...].astype(F32) * hc_ref[...].astype(F32)
    tail = (gct_ref[...].astype(F32) * hct_ref[...].astype(F32))[BF16_ROWS - SUBLANES:]
    tail = jnp.where(i % per_seq == 0, 0.0, tail)
    body, head = _conv3(p, tail, w_ref[...])
    gb = gb_ref[...].astype(F32)
    o_ref[...] = (gb * body).astype(o_ref.dtype)
    o_ref[0:BF16_ROWS, :] = (gb[0:BF16_ROWS] * head).astype(o_ref.dtype)


def _conv_mixer(proj, w_conv, attn_width, conv_width, seq, tm=1024, tc=1024):
    t = proj.shape[0]
    per_seq = seq // tm
    cb = lambda off: (off // tc)
    b_off, c_off, h_off = cb(3 * attn_width), cb(3 * attn_width + conv_width), cb(3 * attn_width + 2 * conv_width)
    tail_blocks = tm // BF16_ROWS
    tail_idx = lambda i: jnp.maximum(i * tail_blocks - 1, 0)
    return pl.pallas_call(
        functools.partial(_convmix_kernel, per_seq=per_seq),
        out_shape=jax.ShapeDtypeStruct((t, conv_width), BF16),
        grid=(t // tm, conv_width // tc),
        in_specs=[pl.BlockSpec((tm, tc), lambda i, j: (i, b_off + j)),
                  pl.BlockSpec((tm, tc), lambda i, j: (i, c_off + j)),
                  pl.BlockSpec((tm, tc), lambda i, j: (i, h_off + j)),
                  pl.BlockSpec((BF16_ROWS, tc), lambda i, j: (tail_idx(i), c_off + j)),
                  pl.BlockSpec((BF16_ROWS, tc), lambda i, j: (tail_idx(i), h_off + j)),
                  pl.BlockSpec((CONV_K, tc), lambda i, j: (0, j))],
        out_specs=pl.BlockSpec((tm, tc), lambda i, j: (i, j)),
        compiler_params=_params("parallel", "parallel"),
        name="conv_mixer",
    )(proj, proj, proj, proj, proj, w_conv)


def _accumulate_dot(acc_ref, a, w_ref, first, k_rows=None):
    n = acc_ref.shape[1]
    for c in range(n // ACC_COLS):
        cols = slice(c * ACC_COLS, (c + 1) * ACC_COLS)
        w = w_ref[:, cols] if k_rows is None else w_ref[0:k_rows, cols]
        part = jnp.dot(a, w, preferred_element_type=F32)
        if first:
            acc_ref[:, cols] = part
        else:
            acc_ref[:, cols] += part


def _stream_epilogue(n_chunks, loads, stores, finish_rows, finish):
    for load in loads:
        load(0, 0).start()
    stats = None
    for r in range(n_chunks + 1):
        new_stats = finish_rows(r) if r < n_chunks else None
        if r >= 1:
            done, slot = r - 1, (r - 1) % 2
            for load in loads:
                load(done, slot).wait()
            if done >= 2:
                for store in stores:
                    store(done - 2, slot).wait()
            finish(done, slot, stats)
            for store in stores:
                store(done, slot).start()
        if r + 1 < n_chunks:
            for load in loads:
                load(r + 1, (r + 1) % 2).start()
        stats = new_stats
    for r in range(max(n_chunks - 2, 0), n_chunks):
        for store in stores:
            store(r, r % 2).wait()


def _row_copy(hbm_ref, buf_ref, sem_ref, row0, to_hbm):
    rows_per = buf_ref.shape[1]

    def make(chunk, slot):
        hbm = hbm_ref.at[pl.ds(row0 + chunk * rows_per, rows_per), :]
        if to_hbm:
            return pltpu.make_async_copy(buf_ref.at[slot], hbm, sem_ref.at[slot])
        return pltpu.make_async_copy(hbm, buf_ref.at[slot], sem_ref.at[slot])
    return make


def _outproj_kernel(a_ref, c_ref, w_ref, gt_ref, gpost_ref, gpre_ref, sc_ref, sh_ref, x_hbm,
                    x1_hbm, h2_hbm, acc_ref, x_buf, x1_buf, h2_buf, x_sem, x1_sem, h2_sem,
                    *, nk_attn):
    i = pl.program_id(0)
    k = pl.program_id(1)
    nk = pl.num_programs(1)
    tm = acc_ref.shape[0]

    @pl.when(k == 0)
    def _():
        _accumulate_dot(acc_ref, a_ref[...], w_ref, first=True)

    @pl.when((k > 0) & (k < nk_attn))
    def _():
        _accumulate_dot(acc_ref, a_ref[...], w_ref, first=False)

    @pl.when((k >= nk_attn) & (k < nk - 1))
    def _():
        _accumulate_dot(acc_ref, c_ref[...], w_ref, first=False)

    @pl.when(k == nk - 1)
    def _():
        row0 = pl.multiple_of(i * tm, tm)

        def finish_rows(r):
            rows = slice(r * EPILOGUE_ROWS, (r + 1) * EPILOGUE_ROWS)
            y = acc_ref[rows, :] + jnp.dot(c_ref[rows, :], w_ref[...],
                                           preferred_element_type=F32)
            acc_ref[rows, :] = y
            return _rms_scale(y)

        post_gain = gt_ref[0] * gpost_ref[...]
        pre_gain = gpre_ref[...] * (1.0 + sc_ref[0])
        shift = sh_ref[0]

        def finish(r, slot, y_scale):
            rows = slice(r * EPILOGUE_ROWS, (r + 1) * EPILOGUE_ROWS)
            x1 = x_buf[slot] + (acc_ref[rows, :] * y_scale) * post_gain
            x1_buf[slot] = x1
            h2_buf[slot] = ((x1 * _rms_scale(x1)) * pre_gain + shift).astype(h2_buf.dtype)

        _stream_epilogue(tm // EPILOGUE_ROWS,
                         [_row_copy(x_hbm, x_buf, x_sem, row0, to_hbm=False)],
                         [_row_copy(x1_hbm, x1_buf, x1_sem, row0, to_hbm=True),
                          _row_copy(h2_hbm, h2_buf, h2_sem, row0, to_hbm=True)],
                         finish_rows, finish)


def _outproj(attn, conv, w_out, x2, gt, g_post, g_pre, sc, sh, seq, tm=1024, tk=512):
    t, d = x2.shape
    ka = attn.shape[1]
    nk_attn = ka // tk
    nk = d // tk
    assert nk - 1 > nk_attn >= 1
    per_seq = seq // tm
    row = lambda i, k: (0, 0)
    mod = lambda i, k: (i // per_seq, 0, 0)
    hbm = pl.BlockSpec(memory_space=pl.ANY)
    return pl.pallas_call(
        functools.partial(_outproj_kernel, nk_attn=nk_attn),
        out_shape=(jax.ShapeDtypeStruct((t, d), F32), jax.ShapeDtypeStruct((t, d), BF16)),
        grid=(t // tm, nk),
        in_specs=[pl.BlockSpec((tm, tk), lambda i, k: (i, jnp.minimum(k, nk_attn - 1))),
                  pl.BlockSpec((tm, tk), lambda i, k: (i, jnp.maximum(k - nk_attn, 0))),
                  pl.BlockSpec((tk, d), lambda i, k: (k, 0)),
                  pl.BlockSpec((1, 1, d), mod),
                  pl.BlockSpec((1, d), row),
                  pl.BlockSpec((1, d), row),
                  pl.BlockSpec((1, 1, d), mod),
                  pl.BlockSpec((1, 1, d), mod),
                  hbm],
        out_specs=(hbm, hbm),
        scratch_shapes=[pltpu.VMEM((tm, d), F32),
                        pltpu.VMEM((2, EPILOGUE_ROWS, d), F32),
                        pltpu.VMEM((2, EPILOGUE_ROWS, d), F32),
                        pltpu.VMEM((2, EPILOGUE_ROWS, d), BF16),
                        pltpu.SemaphoreType.DMA((2,)),
                        pltpu.SemaphoreType.DMA((2,)),
                        pltpu.SemaphoreType.DMA((2,))],
        compiler_params=_params("parallel", "arbitrary"),
        name="outproj_norms",
    )(attn, conv, w_out, gt, g_post.reshape(1, d), g_pre.reshape(1, d), sc, sh, x2)


def _upproj_kernel(h_ref, wg_ref, wv_ref, cg_ref, cv_ref, o_ref, tail_g, tail_v, *, per_seq):
    i = pl.program_id(0)
    j = pl.program_id(1)

    @pl.when(i % per_seq == 0)
    def _():
        tail_g[j] = jnp.zeros(tail_g.shape[1:], F32)
        tail_v[j] = jnp.zeros(tail_v.shape[1:], F32)

    tg = tail_g[j]
    tv = tail_v[j]
    cg = cg_ref[...]
    cv = cv_ref[...]
    for r in range(o_ref.shape[0] // DOT_ROWS):
        rows = slice(r * DOT_ROWS, (r + 1) * DOT_ROWS)
        h = h_ref[rows, :]
        ug = jnp.dot(h, wg_ref[...], preferred_element_type=F32)
        uv = jnp.dot(h, wv_ref[...], preferred_element_type=F32)
        g_body, g_head = _conv3(ug, tg, cg)
        v_body, v_head = _conv3(uv, tv, cv)
        tg = ug[DOT_ROWS - SUBLANES:]
        tv = uv[DOT_ROWS - SUBLANES:]
        o_ref[rows, :] = (g_body * jax.nn.sigmoid(g_body) * v_body).astype(o_ref.dtype)
        head = slice(r * DOT_ROWS, r * DOT_ROWS + BF16_ROWS)
        o_ref[head, :] = (g_head * jax.nn.sigmoid(g_head) * v_head).astype(o_ref.dtype)
    tail_g[j] = tg
    tail_v[j] = tv


def _column_blocks(w, tn):
    d, n = w.shape
    return w.astype(BF16).reshape(d, n // tn, tn).transpose(1, 0, 2)


def _upproj(h2, w_up, w_conv, d_ff, seq, tm=2048, tn=256):
    t, d = h2.shape
    nj = d_ff // tn
    per_seq = seq // tm
    return pl.pallas_call(
        functools.partial(_upproj_kernel, per_seq=per_seq),
        out_shape=jax.ShapeDtypeStruct((t, d_ff), BF16),
        grid=(t // tm, nj),
        in_specs=[pl.BlockSpec((tm, d), lambda i, j: (i, 0)),
                  pl.BlockSpec((None, d, tn), lambda i, j: (j, 0, 0)),
                  pl.BlockSpec((None, d, tn), lambda i, j: (nj + j, 0, 0)),
                  pl.BlockSpec((CONV_K, tn), lambda i, j: (0, j)),
                  pl.BlockSpec((CONV_K, tn), lambda i, j: (0, nj + j))],
        out_specs=pl.BlockSpec((tm, tn), lambda i, j: (i, j)),
        scratch_shapes=[pltpu.VMEM((nj, SUBLANES, tn), F32),
                        pltpu.VMEM((nj, SUBLANES, tn), F32)],
        compiler_params=_params("arbitrary", "arbitrary"),
        name="upproj_conv_gate",
    )(h2, w_up, w_up, w_conv, w_conv)


def _downproj_kernel(g_ref, w_ref, gt_ref, gpost_ref, x1_hbm, o_hbm, acc_ref, x1_buf, o_buf,
                     x1_sem, o_sem, *, nk, k_last):
    i = pl.program_id(0)
    k = pl.program_id(1)
    tm = acc_ref.shape[0]

    @pl.when(k == 0)
    def _():
        _accumulate_dot(acc_ref, g_ref[...], w_ref, first=True)

    @pl.when((k > 0) & (k < nk - 1))
    def _():
        _accumulate_dot(acc_ref, g_ref[...], w_ref, first=False)

    @pl.when(k == nk - 1)
    def _():
        row0 = pl.multiple_of(i * tm, tm)

        def finish_rows(r):
            rows = slice(r * EPILOGUE_ROWS, (r + 1) * EPILOGUE_ROWS)
            f = acc_ref[rows, :] + jnp.dot(g_ref[rows, 0:k_last], w_ref[0:k_last, :],
                                           preferred_element_type=F32)
            acc_ref[rows, :] = f
            return _rms_scale(f)

        post_gain = gt_ref[0] * gpost_ref[...]

        def finish(r, slot, f_scale):
            rows = slice(r * EPILOGUE_ROWS, (r + 1) * EPILOGUE_ROWS)
            o_buf[slot] = x1_buf[slot] + (acc_ref[rows, :] * f_scale) * post_gain

        _stream_epilogue(tm // EPILOGUE_ROWS,
                         [_row_copy(x1_hbm, x1_buf, x1_sem, row0, to_hbm=False)],
                         [_row_copy(o_hbm, o_buf, o_sem, row0, to_hbm=True)],
                         finish_rows, finish)


def _downproj(g, w_down, x1, gt, g_post, seq, tm=1024, tk=1024):
    t, d = x1.shape
    d_ff = g.shape[1]
    nk = pl.cdiv(d_ff, tk)
    k_last = d_ff - (nk - 1) * tk
    assert nk >= 3
    per_seq = seq // tm
    hbm = pl.BlockSpec(memory_space=pl.ANY)
    return pl.pallas_call(
        functools.partial(_downproj_kernel, nk=nk, k_last=k_last),
        out_shape=jax.ShapeDtypeStruct((t, d), F32),
        grid=(t // tm, nk),
        in_specs=[pl.BlockSpec((tm, tk), lambda i, k: (i, k)),
                  pl.BlockSpec((tk, d), lambda i, k: (k, 0)),
                  pl.BlockSpec((1, 1, d), lambda i, k: (i // per_seq, 0, 0)),
                  pl.BlockSpec((1, d), lambda i, k: (0, 0)),
                  hbm],
        out_specs=hbm,
        scratch_shapes=[pltpu.VMEM((tm, d), F32),
                        pltpu.VMEM((2, EPILOGUE_ROWS, d), F32),
                        pltpu.VMEM((2, EPILOGUE_ROWS, d), F32),
                        pltpu.SemaphoreType.DMA((2,)),
                        pltpu.SemaphoreType.DMA((2,))],
        compiler_params=_params("parallel", "arbitrary"),
        name="downproj_norm",
    )(g, w_down, gt, g_post.reshape(1, d), x1)


def kernel(x, c, positions, w_ada, b_ada, g_pre_mix, g_post_mix, w_in, lambda_q1, lambda_k1, lambda_q2, lambda_k2, g_subln, w_conv_mix, w_out, g_pre_ffn, g_post_ffn, w_up, w_conv_ffn, w_down):
    batch, seq, d = x.shape
    depth = w_ada.shape[0]
    assert depth == 1, "the attention lambda schedule below is written for a single layer"
    attn_width = ATTN_HEADS * 2 * ATTN_HEAD_DIM
    conv_width = d - attn_width
    d_ff = w_down.shape[1]
    t = batch * seq

    cos, sin = _rope_tables(positions)
    x2 = x.reshape(t, d)
    for l in range(depth):
        mod = _ada_modulation(c, w_ada[l], b_ada[l])
        sh1, sc1, gt1, sh2, sc2, gt2 = [m.reshape(batch, 1, d) for m in jnp.split(mod, N_MOD, axis=-1)]

        h = _prenorm(x2, g_pre_mix[l], sc1, sh1, seq)
        proj = _inproj(h, _column_blocks(w_in[l], IN_COLS), cos, sin, attn_width)
        attn = _diff_attention(proj, lambda_q1[l], lambda_k1[l], lambda_q2[l], lambda_k2[l],
                               g_subln[l], batch, seq)
        conv = _conv_mixer(proj, w_conv_mix[l], attn_width, conv_width, seq)
        x1, h2 = _outproj(attn, conv, w_out[l].astype(BF16), x2, gt1, g_post_mix[l], g_pre_ffn[l],
                          sc2, sh2, seq)
        gated = _upproj(h2, _column_blocks(w_up[l], UP_COLS), w_conv_ffn[l], d_ff, seq, tn=UP_COLS)
        x2 = _downproj(gated, w_down[l].astype(BF16), x1, gt2, g_post_ffn[l], seq)
    return x2.reshape(batch, seq, d)
```

```python
import functools
import math

import jax
import jax.numpy as jnp
from jax import lax
from jax.experimental import pallas as pl
from jax.experimental.pallas import tpu as pltpu

F32 = jnp.float32
BF16 = jnp.bfloat16

ATTN_HEADS = 8
ATTN_HEAD_DIM = 128
CONV_K = 3
ROPE_THETA = 10000.0
EPS = 1e-6
N_MOD = 6
LAM_INIT = 0.8 - 0.6 * math.exp(-0.3 * 0)

LANES = 128
SUBLANES = 8
BF16_ROWS = 16
VMEM_LIMIT = 56 * 1024 * 1024
ACC_COLS = 1024
EPILOGUE_ROWS = 128
DOT_ROWS = 256
ATTN_Q_COLS = 256


def _params(*sem):
    return pltpu.CompilerParams(dimension_semantics=sem, vmem_limit_bytes=VMEM_LIMIT)


def _rms_scale(v):
    return lax.rsqrt(jnp.mean(v * v, axis=-1, keepdims=True) + EPS)


def _ada_kernel(c_ref, w_ref, b_ref, o_ref):
    c = c_ref[...]
    cond = c * jax.nn.sigmoid(c)
    o_ref[...] = jnp.dot(cond.astype(BF16), w_ref[...].astype(BF16),
                         preferred_element_type=F32) + b_ref[...]


def _ada_modulation(c, w_ada, b_ada, tn=512):
    b, d = c.shape
    n = w_ada.shape[1]
    rows = -(-b // SUBLANES) * SUBLANES
    c_pad = jnp.pad(c, ((0, rows - b), (0, 0)))
    out = pl.pallas_call(
        _ada_kernel,
        out_shape=jax.ShapeDtypeStruct((rows, n), F32),
        grid=(n // tn,),
        in_specs=[pl.BlockSpec((rows, d), lambda j: (0, 0)),
                  pl.BlockSpec((d, tn), lambda j: (0, j)),
                  pl.BlockSpec((1, tn), lambda j: (0, j))],
        out_specs=pl.BlockSpec((rows, tn), lambda j: (0, j)),
        compiler_params=_params("parallel"),
        name="ada_modulation",
    )(c_pad, w_ada, b_ada.reshape(1, n))
    return out[:b]


def _rope_kernel(pos_ref, inv_ref, cos_ref, sin_ref):
    ang = pos_ref[...].astype(F32) * inv_ref[...]
    lane = lax.broadcasted_iota(jnp.int32, ang.shape, 1)
    sin = jnp.sin(ang)
    cos_ref[...] = jnp.cos(ang)
    sin_ref[...] = jnp.where(lane < ATTN_HEAD_DIM // 2, -sin, sin)


def _rope_tables(positions, tm=2048):
    t = positions.size
    dim = ATTN_HEAD_DIM
    inv_freq = ROPE_THETA ** (-jnp.arange(0, dim, 2, dtype=F32) / dim)
    inv = jnp.concatenate([inv_freq, inv_freq]).reshape(1, dim)
    return pl.pallas_call(
        _rope_kernel,
        out_shape=(jax.ShapeDtypeStruct((t, dim), F32), jax.ShapeDtypeStruct((t, dim), F32)),
        grid=(t // tm,),
        in_specs=[pl.BlockSpec((tm, 1), lambda i: (i, 0)),
                  pl.BlockSpec((1, dim), lambda i: (0, 0))],
        out_specs=(pl.BlockSpec((tm, dim), lambda i: (i, 0)),
                   pl.BlockSpec((tm, dim), lambda i: (i, 0))),
        compiler_params=_params("parallel"),
        name="rope_tables",
    )(positions.reshape(t, 1), inv)


def _prenorm_kernel(x_ref, g_ref, sc_ref, sh_ref, o_ref):
    x = x_ref[...]
    y = x * _rms_scale(x) * g_ref[...]
    o_ref[...] = (y * (1.0 + sc_ref[0]) + sh_ref[0]).astype(o_ref.dtype)


def _prenorm(x2, g, sc, sh, seq, tm=512):
    t, d = x2.shape
    per_seq = seq // tm
    return pl.pallas_call(
        _prenorm_kernel,
        out_shape=jax.ShapeDtypeStruct((t, d), BF16),
        grid=(t // tm,),
        in_specs=[pl.BlockSpec((tm, d), lambda i: (i, 0)),
                  pl.BlockSpec((1, d), lambda i: (0, 0)),
                  pl.BlockSpec((1, 1, d), lambda i: (i // per_seq, 0, 0)),
                  pl.BlockSpec((1, 1, d), lambda i: (i // per_seq, 0, 0))],
        out_specs=pl.BlockSpec((tm, d), lambda i: (i, 0)),
        compiler_params=_params("parallel"),
        name="prenorm",
    )(x2, g.reshape(1, d), sc, sh)


def _inproj_kernel(h_ref, w_ref, cos_ref, sin_ref, o_ref, *, n_q, n_qk, q_scale):
    j = pl.program_id(1)
    tm, tn = o_ref.shape

    @pl.when(j >= n_qk)
    def _():
        for r in range(tm // DOT_ROWS):
            rows = slice(r * DOT_ROWS, (r + 1) * DOT_ROWS)
            acc = jnp.dot(h_ref[rows, :], w_ref[...], preferred_element_type=F32)
            o_ref[rows, :] = acc.astype(o_ref.dtype)

    @pl.when(j < n_qk)
    def _():
        scale = jnp.where(j < n_q, q_scale, 1.0).astype(F32)
        for r in range(tm // DOT_ROWS):
            rows = slice(r * DOT_ROWS, (r + 1) * DOT_ROWS)
            acc = jnp.dot(h_ref[rows, :], w_ref[...], preferred_element_type=F32)
            cos = cos_ref[rows, :]
            sin = sin_ref[rows, :] * scale
            cos = cos * scale
            for c in range(tn // LANES):
                t = acc[:, c * LANES:(c + 1) * LANES]
                rot = t * cos + pltpu.roll(t, ATTN_HEAD_DIM // 2, 1) * sin
                o_ref[rows, c * LANES:(c + 1) * LANES] = rot.astype(o_ref.dtype)


def _inproj(h, w_in, cos, sin, attn_width, tm=1024, tn=1024):
    t, d = h.shape
    n = w_in.shape[1]
    kern = functools.partial(_inproj_kernel, n_q=attn_width // tn, n_qk=2 * attn_width // tn,
                             q_scale=ATTN_HEAD_DIM ** -0.5 * math.log2(math.e))
    return pl.pallas_call(
        kern,
        out_shape=jax.ShapeDtypeStruct((t, n), BF16),
        grid=(t // tm, n // tn),
        in_specs=[pl.BlockSpec((tm, d), lambda i, j: (i, 0)),
                  pl.BlockSpec((d, tn), lambda i, j: (0, j)),
                  pl.BlockSpec((tm, ATTN_HEAD_DIM), lambda i, j: (i, 0)),
                  pl.BlockSpec((tm, ATTN_HEAD_DIM), lambda i, j: (i, 0))],
        out_specs=pl.BlockSpec((tm, tn), lambda i, j: (i, j)),
        compiler_params=_params("parallel", "parallel"),
        name="inproj_rope",
    )(h, w_in, cos, sin)


def _attn_kernel(q_ref, k_ref, v_ref, lq1_ref, lk1_ref, lq2_ref, lk2_ref, g_ref,
                 o_ref, vt_ref, acc_ref, sa_ref, sb_ref):
    qi = pl.program_id(2)
    tq = q_ref.shape[0]
    tk = vt_ref.shape[2]
    d = ATTN_HEAD_DIM
    diag_chunks = tq // tk
    first_diag = qi * diag_chunks

    for t in range(diag_chunks):
        rows = pl.ds(pl.multiple_of((first_diag + t) * tk, tk), tk)
        vt_ref[first_diag + t] = v_ref[rows, :].T
    q_t = q_ref[...].T
    acc_ref[...] = jnp.zeros(acc_ref.shape, F32)

    chains = [(c, h) for c in range(2) for h in range(tq // ATTN_Q_COLS)]

    def live_chains(diag):
        if diag is None:
            return {n: False for n in range(len(chains))}
        return {n: h * ATTN_Q_COLS < (diag + 1) * tk
                for n, (c, h) in enumerate(chains) if (h + 1) * ATTN_Q_COLS > diag * tk}

    def score_dots(j, s_ref, diag=None):
        k = k_ref[pl.ds(pl.multiple_of(j * tk, tk), tk), :]
        for n in live_chains(diag):
            c, h = chains[n]
            cols = slice(h * ATTN_Q_COLS, (h + 1) * ATTN_Q_COLS)
            s_ref[n] = jnp.dot(k[:, c * d:(c + 1) * d], q_t[c * d:(c + 1) * d, cols],
                               preferred_element_type=F32)

    def fold(j, s_ref, m, l, diag=None):
        v_t = vt_ref[j]
        m, l = list(m), list(l)
        for n, masked in live_chains(diag).items():
            c, h = chains[n]
            cols = slice(h * ATTN_Q_COLS, (h + 1) * ATTN_Q_COLS)
            s = s_ref[n]
            if masked:
                kv = lax.broadcasted_iota(jnp.int32, s.shape, 0) + diag * tk
                qq = lax.broadcasted_iota(jnp.int32, s.shape, 1) + h * ATTN_Q_COLS
                s = jnp.where(kv <= qq, s, jnp.finfo(F32).min)
            m_new = jnp.maximum(m[n], jnp.max(s, axis=0, keepdims=True))
            alpha = jnp.exp2(m[n] - m_new)
            p = jnp.exp2(s - m_new)
            l[n] = alpha * l[n] + jnp.sum(p, axis=0, keepdims=True)
            acc_ref[c, :, cols] = alpha * acc_ref[c, :, cols] + jnp.dot(
                v_t, p.astype(BF16), preferred_element_type=F32)
            m[n] = m_new
        return tuple(m), tuple(l)

    neg = jnp.full((1, ATTN_Q_COLS), -jnp.inf, F32)
    zero = jnp.zeros((1, ATTN_Q_COLS), F32)

    def pair(i, ml):
        m, l = ml
        score_dots(2 * i + 1, sb_ref)
        m, l = fold(2 * i, sa_ref, m, l)
        score_dots(2 * i + 2, sa_ref)
        return fold(2 * i + 1, sb_ref, m, l)

    score_dots(0, sa_ref)
    m, l = lax.fori_loop(0, first_diag // 2, pair,
                         ((neg,) * len(chains), (zero,) * len(chains)))
    for t in range(diag_chunks):
        cur, nxt = (sa_ref, sb_ref) if t % 2 == 0 else (sb_ref, sa_ref)
        if t + 1 < diag_chunks:
            score_dots(first_diag + t + 1, nxt, diag=t + 1)
        m, l = fold(first_diag + t, cur, m, l, diag=t)
    l = [jnp.concatenate([l[n] for n, (c, h) in enumerate(chains) if c == cc], axis=1)
         for cc in range(2)]

    lam = (jnp.exp(jnp.sum(lq1_ref[...] * lk1_ref[...], axis=-1, keepdims=True))
           - jnp.exp(jnp.sum(lq2_ref[...] * lk2_ref[...], axis=-1, keepdims=True)) + LAM_INIT)
    o_t = acc_ref[0] * (1.0 / l[0]) - acc_ref[1] * (lam / l[1])
    o_t = o_t * lax.rsqrt(jnp.mean(o_t * o_t, axis=0, keepdims=True) + EPS)
    o_ref[...] = (o_t.T * (g_ref[...] * (1.0 - LAM_INIT))).astype(o_ref.dtype)


def _diff_attention(proj, lq1, lk1, lq2, lk2, g_subln, batch, seq, tq=2048, tk=512):
    t = proj.shape[0]
    hw = 2 * ATTN_HEAD_DIM
    nq = seq // tq
    assert tq % (2 * tk) == 0 and tk % ATTN_Q_COLS == 0
    n_chains = 2 * tq // ATTN_Q_COLS
    k_off = ATTN_HEADS
    v_off = 2 * ATTN_HEADS
    vec = lambda a: a.reshape(1, -1)
    const = lambda b, h, i: (0, 0)
    return pl.pallas_call(
        _attn_kernel,
        out_shape=jax.ShapeDtypeStruct((t, ATTN_HEADS * hw), BF16),
        grid=(batch, ATTN_HEADS, nq),
        in_specs=[pl.BlockSpec((tq, hw), lambda b, h, i: (b * nq + i, h)),
                  pl.BlockSpec((seq, hw), lambda b, h, i: (b, k_off + h)),
                  pl.BlockSpec((seq, hw), lambda b, h, i: (b, v_off + h)),
                  pl.BlockSpec((1, ATTN_HEAD_DIM), const),
                  pl.BlockSpec((1, ATTN_HEAD_DIM), const),
                  pl.BlockSpec((1, ATTN_HEAD_DIM), const),
                  pl.BlockSpec((1, ATTN_HEAD_DIM), const),
                  pl.BlockSpec((1, hw), const)],
        out_specs=pl.BlockSpec((tq, hw), lambda b, h, i: (b * nq + i, h)),
        scratch_shapes=[pltpu.VMEM((seq // tk, hw, tk), BF16),
                        pltpu.VMEM((2, hw, tq), F32),
                        pltpu.VMEM((n_chains, tk, ATTN_Q_COLS), F32),
                        pltpu.VMEM((n_chains, tk, ATTN_Q_COLS), F32)],
        compiler_params=_params("parallel", "parallel", "arbitrary"),
        name="diff_attention",
    )(proj, proj, proj, vec(lq1), vec(lk1), vec(lq2), vec(lk2), vec(g_subln))


def _conv3(p, tail, w):
    w0, w1, w2 = w[0:1], w[1:2], w[2:3]
    body = pltpu.roll(p, 2, 0) * w0 + pltpu.roll(p, 1, 0) * w1 + p * w2
    first = p[0:SUBLANES]
    row = lax.broadcasted_iota(jnp.int32, first.shape, 0)
    h1 = jnp.where(row < 1, pltpu.roll(tail, 1, 0), pltpu.roll(first, 1, 0))
    h2 = jnp.where(row < 2, pltpu.roll(tail, 2, 0), pltpu.roll(first, 2, 0))
    head = jnp.concatenate([h2 * w0 + h1 * w1 + first * w2, body[SUBLANES:BF16_ROWS]], axis=0)
    return body, head


def _convmix_kernel(gb_ref, gc_ref, hc_ref, gct_ref, hct_ref, w_ref, o_ref, *, per_seq):
    i = pl.program_id(0)
    p = gc_ref[...].astype(F32) * hc_ref[...].astype(F32)
    tail = (gct_ref[...].astype(F32) * hct_ref[...].astype(F32))[BF16_ROWS - SUBLANES:]
    tail = jnp.where(i % per_seq == 0, 0.0, tail)
    body, head = _conv3(p, tail, w_ref[...])
    gb = gb_ref[...].astype(F32)
    o_ref[...] = (gb * body).astype(o_ref.dtype)
    o_ref[0:BF16_ROWS, :] = (gb[0:BF16_ROWS] * head).astype(o_ref.dtype)


def _conv_mixer(proj, w_conv, attn_width, conv_width, seq, tm=1024, tc=1024):
    t = proj.shape[0]
    per_seq = seq // tm
    cb = lambda off: (off // tc)
    b_off, c_off, h_off = cb(3 * attn_width), cb(3 * attn_width + conv_width), cb(3 * attn_width + 2 * conv_width)
    tail_blocks = tm // BF16_ROWS
    tail_idx = lambda i: jnp.maximum(i * tail_blocks - 1, 0)
    return pl.pallas_call(
        functools.partial(_convmix_kernel, per_seq=per_seq),
        out_shape=jax.ShapeDtypeStruct((t, conv_width), BF16),
        grid=(t // tm, conv_width // tc),
        in_specs=[pl.BlockSpec((tm, tc), lambda i, j: (i, b_off + j)),
                  pl.BlockSpec((tm, tc), lambda i, j: (i, c_off + j)),
                  pl.BlockSpec((tm, tc), lambda i, j: (i, h_off + j)),
                  pl.BlockSpec((BF16_ROWS, tc), lambda i, j: (tail_idx(i), c_off + j)),
                  pl.BlockSpec((BF16_ROWS, tc), lambda i, j: (tail_idx(i), h_off + j)),
                  pl.BlockSpec((CONV_K, tc), lambda i, j: (0, j))],
        out_specs=pl.BlockSpec((tm, tc), lambda i, j: (i, j)),
        compiler_params=_params("parallel", "parallel"),
        name="conv_mixer",
    )(proj, proj, proj, proj, proj, w_conv)


def _accumulate_dot(acc_ref, a, w_ref, first, k_rows=None):
    n = acc_ref.shape[1]
    for c in range(n // ACC_COLS):
        cols = slice(c * ACC_COLS, (c + 1) * ACC_COLS)
        w = w_ref[:, cols] if k_rows is None else w_ref[0:k_rows, cols]
        part = jnp.dot(a, w, preferred_element_type=F32)
        if first:
            acc_ref[:, cols] = part
        else:
            acc_ref[:, cols] += part


def _stream_epilogue(n_chunks, loads, stores, finish_rows, finish):
    for load in loads:
        load(0, 0).start()
    stats = None
    for r in range(n_chunks + 1):
        new_stats = finish_rows(r) if r < n_chunks else None
        if r >= 1:
            done, slot = r - 1, (r - 1) % 2
            for load in loads:
                load(done, slot).wait()
            if done >= 2:
                for store in stores:
                    store(done - 2, slot).wait()
            finish(done, slot, stats)
            for store in stores:
                store(done, slot).start()
        if r + 1 < n_chunks:
            for load in loads:
                load(r + 1, (r + 1) % 2).start()
        stats = new_stats
    for r in range(max(n_chunks - 2, 0), n_chunks):
        for store in stores:
            store(r, r % 2).wait()


def _row_copy(hbm_ref, buf_ref, sem_ref, row0, to_hbm):
    rows_per = buf_ref.shape[1]

    def make(chunk, slot):
        hbm = hbm_ref.at[pl.ds(row0 + chunk * rows_per, rows_per), :]
        if to_hbm:
            return pltpu.make_async_copy(buf_ref.at[slot], hbm, sem_ref.at[slot])
        return pltpu.make_async_copy(hbm, buf_ref.at[slot], sem_ref.at[slot])
    return make


def _outproj_kernel(a_ref, c_ref, w_ref, gt_ref, gpost_ref, gpre_ref, sc_ref, sh_ref, x_hbm,
                    x1_hbm, h2_hbm, acc_ref, x_buf, x1_buf, h2_buf, x_sem, x1_sem, h2_sem,
                    *, nk_attn):
    i = pl.program_id(0)
    k = pl.program_id(1)
    nk = pl.num_programs(1)
    tm = acc_ref.shape[0]

    @pl.when(k == 0)
    def _():
        _accumulate_dot(acc_ref, a_ref[...], w_ref, first=True)

    @pl.when((k > 0) & (k < nk_attn))
    def _():
        _accumulate_dot(acc_ref, a_ref[...], w_ref, first=False)

    @pl.when((k >= nk_attn) & (k < nk - 1))
    def _():
        _accumulate_dot(acc_ref, c_ref[...], w_ref, first=False)

    @pl.when(k == nk - 1)
    def _():
        row0 = pl.multiple_of(i * tm, tm)

        def finish_rows(r):
            rows = slice(r * EPILOGUE_ROWS, (r + 1) * EPILOGUE_ROWS)
            y = acc_ref[rows, :] + jnp.dot(c_ref[rows, :], w_ref[...],
                                           preferred_element_type=F32)
            acc_ref[rows, :] = y
            return _rms_scale(y)

        post_gain = gt_ref[0] * gpost_ref[...]
        pre_gain = gpre_ref[...] * (1.0 + sc_ref[0])
        shift = sh_ref[0]

        def finish(r, slot, y_scale):
            rows = slice(r * EPILOGUE_ROWS, (r + 1) * EPILOGUE_ROWS)
            x1 = x_buf[slot] + (acc_ref[rows, :] * y_scale) * post_gain
            x1_buf[slot] = x1
            h2_buf[slot] = ((x1 * _rms_scale(x1)) * pre_gain + shift).astype(h2_buf.dtype)

        _stream_epilogue(tm // EPILOGUE_ROWS,
                         [_row_copy(x_hbm, x_buf, x_sem, row0, to_hbm=False)],
                         [_row_copy(x1_hbm, x1_buf, x1_sem, row0, to_hbm=True),
                          _row_copy(h2_hbm, h2_buf, h2_sem, row0, to_hbm=True)],
                         finish_rows, finish)


def _outproj(attn, conv, w_out, x2, gt, g_post, g_pre, sc, sh, seq, tm=1024, tk=512):
    t, d = x2.shape
    ka = attn.shape[1]
    nk_attn = ka // tk
    nk = d // tk
    assert nk - 1 > nk_attn >= 1
    per_seq = seq // tm
    row = lambda i, k: (0, 0)
    mod = lambda i, k: (i // per_seq, 0, 0)
    hbm = pl.BlockSpec(memory_space=pl.ANY)
    return pl.pallas_call(
        functools.partial(_outproj_kernel, nk_attn=nk_attn),
        out_shape=(jax.ShapeDtypeStruct((t, d), F32), jax.ShapeDtypeStruct((t, d), BF16)),
        grid=(t // tm, nk),
        in_specs=[pl.BlockSpec((tm, tk), lambda i, k: (i, jnp.minimum(k, nk_attn - 1))),
                  pl.BlockSpec((tm, tk), lambda i, k: (i, jnp.maximum(k - nk_attn, 0))),
                  pl.BlockSpec((tk, d), lambda i, k: (k, 0)),
                  pl.BlockSpec((1, 1, d), mod),
                  pl.BlockSpec((1, d), row),
                  pl.BlockSpec((1, d), row),
                  pl.BlockSpec((1, 1, d), mod),
                  pl.BlockSpec((1, 1, d), mod),
                  hbm],
        out_specs=(hbm, hbm),
        scratch_shapes=[pltpu.VMEM((tm, d), F32),
                        pltpu.VMEM((2, EPILOGUE_ROWS, d), F32),
                        pltpu.VMEM((2, EPILOGUE_ROWS, d), F32),
                        pltpu.VMEM((2, EPILOGUE_ROWS, d), BF16),
                        pltpu.SemaphoreType.DMA((2,)),
                        pltpu.SemaphoreType.DMA((2,)),
                        pltpu.SemaphoreType.DMA((2,))],
        compiler_params=_params("parallel", "arbitrary"),
        name="outproj_norms",
    )(attn, conv, w_out, gt, g_post.reshape(1, d), g_pre.reshape(1, d), sc, sh, x2)


def _upproj_kernel(h_ref, wg32_ref, wv32_ref, cg_ref, cv_ref, o_ref, tail_g, tail_v, wg_ref, wv_ref,
                   *, per_seq):
    i = pl.program_id(0)
    j = pl.program_id(1)
    wg_ref[...] = wg32_ref[...].astype(BF16)
    wv_ref[...] = wv32_ref[...].astype(BF16)

    @pl.when(i % per_seq == 0)
    def _():
        tail_g[j] = jnp.zeros(tail_g.shape[1:], F32)
        tail_v[j] = jnp.zeros(tail_v.shape[1:], F32)

    tg = tail_g[j]
    tv = tail_v[j]
    cg = cg_ref[...]
    cv = cv_ref[...]
    for r in range(o_ref.shape[0] // DOT_ROWS):
        rows = slice(r * DOT_ROWS, (r + 1) * DOT_ROWS)
        h = h_ref[rows, :]
        ug = jnp.dot(h, wg_ref[...], preferred_element_type=F32)
        uv = jnp.dot(h, wv_ref[...], preferred_element_type=F32)
        g_body, g_head = _conv3(ug, tg, cg)
        v_body, v_head = _conv3(uv, tv, cv)
        tg = ug[DOT_ROWS - SUBLANES:]
        tv = uv[DOT_ROWS - SUBLANES:]
        o_ref[rows, :] = (g_body * jax.nn.sigmoid(g_body) * v_body).astype(o_ref.dtype)
        head = slice(r * DOT_ROWS, r * DOT_ROWS + BF16_ROWS)
        o_ref[head, :] = (g_head * jax.nn.sigmoid(g_head) * v_head).astype(o_ref.dtype)
    tail_g[j] = tg
    tail_v[j] = tv


def _upproj(h2, w_up, w_conv, d_ff, seq, tm=2048, tn=256):
    t, d = h2.shape
    nj = d_ff // tn
    per_seq = seq // tm
    return pl.pallas_call(
        functools.partial(_upproj_kernel, per_seq=per_seq),
        out_shape=jax.ShapeDtypeStruct((t, d_ff), BF16),
        grid=(t // tm, nj),
        in_specs=[pl.BlockSpec((tm, d), lambda i, j: (i, 0), pipeline_mode=pl.Buffered(1)),
                  pl.BlockSpec((d, tn), lambda i, j: (0, j)),
                  pl.BlockSpec((d, tn), lambda i, j: (0, nj + j)),
                  pl.BlockSpec((CONV_K, tn), lambda i, j: (0, j)),
                  pl.BlockSpec((CONV_K, tn), lambda i, j: (0, nj + j))],
        out_specs=pl.BlockSpec((tm, tn), lambda i, j: (i, j)),
        scratch_shapes=[pltpu.VMEM((nj, SUBLANES, tn), F32),
                        pltpu.VMEM((nj, SUBLANES, tn), F32),
                        pltpu.VMEM((d, tn), BF16),
                        pltpu.VMEM((d, tn), BF16)],
        compiler_params=_params("arbitrary", "arbitrary"),
        name="upproj_conv_gate",
    )(h2, w_up, w_up, w_conv, w_conv)


def _downproj_kernel(g_ref, w_ref, gt_ref, gpost_ref, x1_hbm, o_hbm, acc_ref, x1_buf, o_buf,
                     x1_sem, o_sem, *, nk, k_last):
    i = pl.program_id(0)
    k = pl.program_id(1)
    tm = acc_ref.shape[0]

    @pl.when(k == 0)
    def _():
        _accumulate_dot(acc_ref, g_ref[...], w_ref, first=True)

    @pl.when((k > 0) & (k < nk - 1))
    def _():
        _accumulate_dot(acc_ref, g_ref[...], w_ref, first=False)

    @pl.when(k == nk - 1)
    def _():
        row0 = pl.multiple_of(i * tm, tm)

        def finish_rows(r):
            rows = slice(r * EPILOGUE_ROWS, (r + 1) * EPILOGUE_ROWS)
            f = acc_ref[rows, :] + jnp.dot(g_ref[rows, 0:k_last], w_ref[0:k_last, :],
                                           preferred_element_type=F32)
            acc_ref[rows, :] = f
            return _rms_scale(f)

        post_gain = gt_ref[0] * gpost_ref[...]

        def finish(r, slot, f_scale):
            rows = slice(r * EPILOGUE_ROWS, (r + 1) * EPILOGUE_ROWS)
            o_buf[slot] = x1_buf[slot] + (acc_ref[rows, :] * f_scale) * post_gain

        _stream_epilogue(tm // EPILOGUE_ROWS,
                         [_row_copy(x1_hbm, x1_buf, x1_sem, row0, to_hbm=False)],
                         [_row_copy(o_hbm, o_buf, o_sem, row0, to_hbm=True)],
                         finish_rows, finish)


def _downproj(g, w_down, x1, gt, g_post, seq, tm=1024, tk=1024):
    t, d = x1.shape
    d_ff = g.shape[1]
    nk = pl.cdiv(d_ff, tk)
    k_last = d_ff - (nk - 1) * tk
    assert nk >= 3
    per_seq = seq // tm
    hbm = pl.BlockSpec(memory_space=pl.ANY)
    return pl.pallas_call(
        functools.partial(_downproj_kernel, nk=nk, k_last=k_last),
        out_shape=jax.ShapeDtypeStruct((t, d), F32),
        grid=(t // tm, nk),
        in_specs=[pl.BlockSpec((tm, tk), lambda i, k: (i, k)),
                  pl.BlockSpec((tk, d), lambda i, k: (k, 0)),
                  pl.BlockSpec((1, 1, d), lambda i, k: (i // per_seq, 0, 0)),
                  pl.BlockSpec((1, d), lambda i, k: (0, 0)),
                  hbm],
        out_specs=hbm,
        scratch_shapes=[pltpu.VMEM((tm, d), F32),
                        pltpu.VMEM((2, EPILOGUE_ROWS, d), F32),
                        pltpu.VMEM((2, EPILOGUE_ROWS, d), F32),
                        pltpu.SemaphoreType.DMA((2,)),
                        pltpu.SemaphoreType.DMA((2,))],
        compiler_params=_params("parallel", "arbitrary"),
        name="downproj_norm",
    )(g, w_down, gt, g_post.reshape(1, d), x1)


def kernel(x, c, positions, w_ada, b_ada, g_pre_mix, g_post_mix, w_in, lambda_q1, lambda_k1, lambda_q2, lambda_k2, g_subln, w_conv_mix, w_out, g_pre_ffn, g_post_ffn, w_up, w_conv_ffn, w_down):
    batch, seq, d = x.shape
    depth = w_ada.shape[0]
    assert depth == 1, "the attention lambda schedule below is written for a single layer"
    attn_width = ATTN_HEADS * 2 * ATTN_HEAD_DIM
    conv_width = d - attn_width
    d_ff = w_down.shape[1]
    t = batch * seq

    cos, sin = _rope_tables(positions)
    x2 = x.reshape(t, d)
    for l in range(depth):
        mod = _ada_modulation(c, w_ada[l], b_ada[l])
        sh1, sc1, gt1, sh2, sc2, gt2 = [m.reshape(batch, 1, d) for m in jnp.split(mod, N_MOD, axis=-1)]

        h = _prenorm(x2, g_pre_mix[l], sc1, sh1, seq)
        proj = _inproj(h, w_in[l].astype(BF16), cos, sin, attn_width)
        attn = _diff_attention(proj, lambda_q1[l], lambda_k1[l], lambda_q2[l], lambda_k2[l],
                               g_subln[l], batch, seq)
        conv = _conv_mixer(proj, w_conv_mix[l], attn_width, conv_width, seq)
        x1, h2 = _outproj(attn, conv, w_out[l].astype(BF16), x2, gt1, g_post_mix[l], g_pre_ffn[l],
                          sc2, sh2, seq)
        gated = _upproj(h2, w_up[l], w_conv_ffn[l], d_ff, seq)
        x2 = _downproj(gated, w_down[l].astype(BF16), x1, gt2, g_post_ffn[l], seq)
    return x2.reshape(batch, seq, d)
```
